```python
import math
import jax
import jax.numpy as jnp
from jax import lax
import numpy as np

D_MODEL = 2048
BATCH = 4
SEQ = 2048
DEPTH = 2
DEC_BATCH = 128
DEC_SEQ = 1
PAST_LEN = 2048
PAGE_SIZE = 128

HEAD_DIM = 128
MIX_W = D_MODEL // 2
N_BRANCH = 3
NSA_H = MIX_W // HEAD_DIM
NSA_KVH = NSA_H // 4
CMP_LEN = 32
CMP_STRIDE = 16
SEL_BLOCK = 64
SEL_TOPK = 16
WINDOW = 512
WIN_QBLOCK = 128
FORCE_BONUS = 1e4
MOBA_H = MIX_W // HEAD_DIM
MOBA_BLOCK = 256
MOBA_TOPK = 3
HG_H = MIX_W // HEAD_DIM
HG_DK = 128
HG_DV = MIX_W // HG_H
HG_CHUNK = 64
D_FF = 4 * D_MODEL
Q_CHUNK = 32
EPS = 1e-6
F_FLOOR = 1e-30
NEG = -1e30

IN_SIZES = (NSA_H * HEAD_DIM, 6 * NSA_KVH * HEAD_DIM, 3 * NSA_H, 3 * MOBA_H * HEAD_DIM,
            HG_H * HG_DK, HG_H * HG_DK, HG_H * HG_DV, HG_H * HG_DV, N_BRANCH * D_MODEL)
D_IN = sum(IN_SIZES)

kernel_name = 'hybrid_nsa_hgrn2_moba_decode_step'


def rmsnorm(x, w):
    xf = x.astype(jnp.float32)
    y = xf * lax.rsqrt(jnp.mean(xf * xf, axis=-1, keepdims=True) + EPS)
    return (y * w.astype(jnp.float32)).astype(x.dtype)


def masked_probs(s, mask):
    p = jax.nn.softmax(jnp.where(mask, s, NEG), axis=-1)
    return jnp.where(mask, p, 0.0)


def map_query_chunks(fn, *xs):
    n_q = xs[0].shape[0]
    qc = math.gcd(n_q, Q_CHUNK)
    n = n_q // qc
    blocks = tuple(a.reshape((n, qc) + a.shape[1:]) for a in xs)
    out = lax.map(lambda a: fn(*a), blocks)
    return out.reshape((n_q,) + out.shape[2:])


def gather_pages(pool, pages):
    rows = pool[pages]
    return rows.reshape((-1,) + pool.shape[2:])


def nsa_compress(x, pos_emb, w1, w2):
    n_blk = (x.shape[0] - CMP_LEN) // CMP_STRIDE + 1
    idx = np.arange(n_blk)[:, None] * CMP_STRIDE + np.arange(CMP_LEN)[None, :]
    blk = x[idx] + pos_emb[None, :, None, :]
    blk = blk.transpose(0, 2, 1, 3).reshape(n_blk, x.shape[1], CMP_LEN * HEAD_DIM)
    return jax.nn.silu(blk @ w1) @ w2


def nsa_seq(q, q_pos, kc, vc, ks, vs, cmp_pos, cmp_w1, cmp_w2):
    n_q, seq_len = q.shape[0], kc.shape[0]
    rep = NSA_H // NSA_KVH
    scale = HEAD_DIM ** -0.5
    qg = q.reshape(n_q, NSA_KVH, rep, HEAD_DIM)
    k_cmp = nsa_compress(kc, cmp_pos[0], cmp_w1[0], cmp_w2[0])
    v_cmp = nsa_compress(vc, cmp_pos[1], cmp_w1[1], cmp_w2[1])
    n_cmp = k_cmp.shape[0]
    cmp_start = np.arange(n_cmp, dtype=np.int32) * CMP_STRIDE
    s = jnp.einsum('qgrd,ngd->qgrn', qg, k_cmp, preferred_element_type=jnp.float32) * scale
    cmask = (cmp_start + CMP_LEN - 1)[None, :] <= q_pos[:, None]
    p_cmp = masked_probs(s, cmask[:, None, None, :])
    o_cmp = jnp.einsum('qgrn,ngd->qgrd', p_cmp.astype(v_cmp.dtype), v_cmp)
    n_sel = -(-seq_len // SEL_BLOCK)
    sel_start = np.arange(n_sel, dtype=np.int32) * SEL_BLOCK
    overlap = ((cmp_start[:, None] < sel_start[None, :] + SEL_BLOCK)
               & (cmp_start[:, None] + CMP_LEN > sel_start[None, :])).astype(np.float32)
    imp = jnp.einsum('qgrn,nj->qgj', p_cmp, overlap)
    cur = q_pos // SEL_BLOCK
    blk = np.arange(n_sel, dtype=np.int32)[None, :]
    valid = blk <= cur[:, None]
    forced = (blk == 0) | (blk == cur[:, None]) | (blk == cur[:, None] - 1)
    score = jnp.where(valid[:, None, :],
                      imp + jnp.where(forced, FORCE_BONUS, 0.0)[:, None, :], NEG)
    n_top = min(SEL_TOPK, n_sel)
    _, idx = lax.top_k(score, n_top)
    ok = np.arange(n_top, dtype=np.int32)[None, :] < jnp.minimum(cur + 1, n_top)[:, None]
    pad = n_sel * SEL_BLOCK - seq_len
    ksb = jnp.pad(ks, ((0, pad), (0, 0), (0, 0))).reshape(
        n_sel, SEL_BLOCK, NSA_KVH, HEAD_DIM).transpose(2, 0, 1, 3)
    vsb = jnp.pad(vs, ((0, pad), (0, 0), (0, 0))).reshape(
        n_sel, SEL_BLOCK, NSA_KVH, HEAD_DIM).transpose(2, 0, 1, 3)
    g_ix = np.arange(NSA_KVH)[None, :, None]
    offs = np.arange(SEL_BLOCK, dtype=np.int32)

    def sel_chunk(q_c, pos_c, idx_c, ok_c):
        n = q_c.shape[0]
        k_g = ksb[g_ix, idx_c]
        v_g = vsb[g_ix, idx_c]
        k_pos = idx_c[..., None] * SEL_BLOCK + offs
        m = (k_pos <= pos_c[:, None, None, None]) & ok_c[:, None, :, None]
        s_c = jnp.einsum('qgrd,qgkbd->qgrkb', q_c, k_g, preferred_element_type=jnp.float32) * scale
        p_c = masked_probs(s_c.reshape(n, NSA_KVH, rep, -1), m.reshape(n, NSA_KVH, 1, -1))
        return jnp.einsum('qgrkb,qgkbd->qgrd', p_c.reshape(s_c.shape).astype(v_g.dtype), v_g)

    o_slc = map_query_chunks(sel_chunk, qg, q_pos, idx, ok)
    return o_cmp.reshape(n_q, NSA_H, HEAD_DIM), o_slc.reshape(n_q, NSA_H, HEAD_DIM)


def window_banded(q, k, v):
    b, t = q.shape[:2]
    rep = NSA_H // NSA_KVH
    qb = math.gcd(t, WIN_QBLOCK)
    nb = t // qb
    span = WINDOW + qb
    idx = np.arange(nb)[:, None] * qb + np.arange(span)[None, :]
    k_pos = idx - WINDOW
    q_pos = np.arange(t).reshape(nb, qb)
    d = q_pos[:, :, None] - k_pos[:, None, :]
    mask = (d >= 0) & (d <= WINDOW) & (k_pos[:, None, :] >= 0)
    kp = jnp.pad(k, ((0, 0), (WINDOW, 0), (0, 0), (0, 0)))[:, idx]
    vp = jnp.pad(v, ((0, 0), (WINDOW, 0), (0, 0), (0, 0)))[:, idx]
    qg = q.reshape(b, nb, qb, NSA_KVH, rep, HEAD_DIM)
    s = jnp.einsum('bnqgrd,bnkgd->bnqgrk', qg, kp, preferred_element_type=jnp.float32) * (HEAD_DIM ** -0.5)
    p = masked_probs(s, mask[None, :, :, None, None, :])
    o = jnp.einsum('bnqgrk,bnkgd->bnqgrd', p.astype(vp.dtype), vp)
    return o.reshape(b, t, NSA_H, HEAD_DIM)


def window_direct(q, q_pos, k, v, k_pos):
    b, n_q = q.shape[:2]
    rep = NSA_H // NSA_KVH
    qg = q.reshape(b, n_q, NSA_KVH, rep, HEAD_DIM)
    s = jnp.einsum('bqgrd,bkgd->bqgrk', qg, k, preferred_element_type=jnp.float32) * (HEAD_DIM ** -0.5)
    d = q_pos[:, None] - k_pos[None, :]
    mask = (d >= 0) & (d <= WINDOW)
    p = masked_probs(s, mask[None, :, None, None, :])
    o = jnp.einsum('bqgrk,bkgd->bqgrd', p.astype(v.dtype), v)
    return o.reshape(b, n_q, NSA_H, HEAD_DIM)


def nsa_combine(gate_logits, o_cmp, o_slc, o_win):
    b, t = o_cmp.shape[:2]
    g = jax.nn.sigmoid(gate_logits.astype(jnp.float32)).reshape(b, t, 3, NSA_H, 1)
    o = g[:, :, 0] * o_cmp + g[:, :, 1] * o_slc + g[:, :, 2] * o_win
    return o.reshape(b, t, MIX_W).astype(o_cmp.dtype)


def moba_seq(q, q_pos, k, v):
    n_q, seq_len = q.shape[0], k.shape[0]
    scale = HEAD_DIM ** -0.5
    nb = -(-seq_len // MOBA_BLOCK)
    pad = nb * MOBA_BLOCK - seq_len
    kb = jnp.pad(k, ((0, pad), (0, 0), (0, 0))).reshape(nb, MOBA_BLOCK, MOBA_H, HEAD_DIM)
    vb = jnp.pad(v, ((0, pad), (0, 0), (0, 0))).reshape(nb, MOBA_BLOCK, MOBA_H, HEAD_DIM)
    k_mean = jnp.mean(kb.astype(jnp.float32), axis=1)
    cur = q_pos // MOBA_BLOCK
    gate = jnp.einsum('qhd,nhd->qhn', q.astype(jnp.float32), k_mean)
    past = np.arange(nb, dtype=np.int32)[None, :] < cur[:, None]
    gate = jnp.where(past[:, None, :], gate, NEG)
    n_top = min(MOBA_TOPK, nb)
    _, idx = lax.top_k(gate, n_top)
    ok = np.arange(n_top, dtype=np.int32)[None, :] < jnp.minimum(cur, n_top)[:, None]
    idx = jnp.concatenate([idx, jnp.broadcast_to(cur[:, None, None], (n_q, MOBA_H, 1))], axis=-1)
    ok = jnp.concatenate([ok, jnp.ones((n_q, 1), dtype=bool)], axis=-1)
    kbh = kb.transpose(2, 0, 1, 3)
    vbh = vb.transpose(2, 0, 1, 3)
    h_ix = np.arange(MOBA_H)[None, :, None]
    offs = np.arange(MOBA_BLOCK, dtype=np.int32)

    def chunk(q_c, pos_c, idx_c, ok_c):
        n = q_c.shape[0]
        k_g = kbh[h_ix, idx_c]
        v_g = vbh[h_ix, idx_c]
        k_pos = idx_c[..., None] * MOBA_BLOCK + offs
        m = (k_pos <= pos_c[:, None, None, None]) & ok_c[:, None, :, None]
        s_c = jnp.einsum('qhd,qhkbd->qhkb', q_c, k_g, preferred_element_type=jnp.float32) * scale
        p_c = masked_probs(s_c.reshape(n, MOBA_H, -1), m.reshape(n, MOBA_H, -1))
        return jnp.einsum('qhkb,qhkbd->qhd', p_c.reshape(s_c.shape).astype(v_g.dtype), v_g)

    return map_query_chunks(chunk, q, q_pos, idx, ok)


def hgrn_features(hq, hf, hi, lb):
    b, t, _ = hq.shape
    q = jax.nn.silu(hq.astype(jnp.float32)).reshape(b, t, HG_H, HG_DK)
    z = hf.astype(jnp.float32).reshape(b, t, HG_H, HG_DK)
    lbh = lb.astype(jnp.float32).reshape(HG_H, HG_DK)
    f = lbh + (1.0 - lbh) * jax.nn.sigmoid(z)
    logf = jnp.log(jnp.maximum(f, F_FLOOR))
    k = (1.0 - lbh) * jax.nn.sigmoid(-z)
    v = hi.astype(jnp.float32).reshape(b, t, HG_H, HG_DV)
    return q, k, logf, v


def hgrn_chunked(q, k, logf, v, s0):
    b, t, h, _ = q.shape
    c = math.gcd(t, HG_CHUNK)
    n = t // c
    tri = np.tril(np.ones((c, c), dtype=bool))[:, :, None]

    def blocks(a):
        return a.reshape(b, n, c, h, a.shape[-1]).transpose(1, 0, 3, 2, 4)

    def step(s, inp):
        q_c, k_c, l_c, v_c = inp
        cum = jnp.cumsum(l_c, axis=2)
        diff = cum[:, :, :, None, :] - cum[:, :, None, :, :]
        decay = jnp.where(tri, jnp.exp(jnp.where(tri, diff, 0.0)), 0.0)
        a = jnp.einsum('bhtd,bhsd,bhtsd->bhts', q_c, k_c, decay)
        o = jnp.einsum('bhts,bhse->bhte', a, v_c) + jnp.einsum('bhtd,bhde->bhte', q_c * jnp.exp(cum), s)
        last = cum[:, :, -1]
        s = jnp.exp(last)[..., None] * s + jnp.einsum(
            'bhsd,bhse->bhde', k_c * jnp.exp(last[:, :, None] - cum), v_c)
        return s, o

    s, o = lax.scan(step, s0, (blocks(q), blocks(k), blocks(logf), blocks(v)))
    return o.transpose(1, 0, 3, 2, 4).reshape(b, t, h, v.shape[-1]), s


def hgrn_recurrent(q, k, logf, v, s0):
    def step(s, inp):
        q_t, k_t, l_t, v_t = inp
        s = jnp.exp(l_t)[..., None] * s + k_t[..., None] * v_t[..., None, :]
        return s, jnp.einsum('bhd,bhde->bhe', q_t, s)

    sw = lambda a: jnp.swapaxes(a, 0, 1)
    s, o = lax.scan(step, s0, (sw(q), sw(k), sw(logf), sw(v)))
    return sw(o), s


def hgrn_output(o, hg, norm_w):
    b, t = o.shape[:2]
    o = o * lax.rsqrt(jnp.mean(o * o, axis=-1, keepdims=True) + EPS)
    o = o * norm_w.astype(jnp.float32).reshape(HG_H, HG_DV)
    g = jax.nn.silu(hg.astype(jnp.float32)).reshape(b, t, HG_H, HG_DV)
    return (o * g).reshape(b, t, MIX_W).astype(hg.dtype)


def split_proj(h, w_in):
    z = jnp.einsum('btd,de->bte', h, w_in)
    return jnp.split(z, np.cumsum(IN_SIZES)[:-1].tolist(), axis=-1)


def finish_layer(x, o_nsa, o_hg, o_moba, merge_logits, w_branch, w_out, norm2_w, w_up, w_down):
    b, t, d = x.shape
    br = jnp.stack([o_nsa, o_hg, o_moba], axis=2)
    proj = jnp.einsum('btnm,nmd->btnd', br, w_branch)
    gate = jax.nn.sigmoid(merge_logits.astype(jnp.float32)).reshape(b, t, N_BRANCH, d)
    mixed = jnp.sum(gate * proj, axis=2).astype(x.dtype)
    x = x + mixed @ w_out
    u = jax.nn.relu(rmsnorm(x, norm2_w) @ w_up)
    return x + (u * u) @ w_down


def prompt_layer(x, lb, lw):
    norm1_w, w_in, cmp_pos, cmp_w1, cmp_w2, hg_norm_w, w_branch, w_out, norm2_w, w_up, w_down = lw
    b, t, _ = x.shape
    h = rmsnorm(x, norm1_w)
    nq, nkv, ngate, mqkv, hq, hf, hi, hg, mg = split_proj(h, w_in)
    pos = jnp.arange(t, dtype=jnp.int32)
    q = nq.reshape(b, t, NSA_H, HEAD_DIM)
    kv = nkv.reshape(b, t, 6, NSA_KVH, HEAD_DIM)
    o_cmp, o_slc = lax.map(lambda a: nsa_seq(a[0], pos, a[1][:, 0], a[1][:, 1], a[1][:, 2], a[1][:, 3],
                                             cmp_pos, cmp_w1, cmp_w2), (q, kv))
    o_win = window_banded(q, kv[:, :, 4], kv[:, :, 5])
    o_nsa = nsa_combine(ngate, o_cmp, o_slc, o_win)
    mqkv = mqkv.reshape(b, t, 3, MOBA_H, HEAD_DIM)
    mq, mk, mv = mqkv[:, :, 0], mqkv[:, :, 1], mqkv[:, :, 2]
    o_moba = lax.map(lambda a: moba_seq(a[0], pos, a[1], a[2]), (mq, mk, mv)).reshape(b, t, MIX_W)
    fq, fk, flogf, fv = hgrn_features(hq, hf, hi, lb)
    s0 = jnp.zeros((b, HG_H, HG_DK, HG_DV), jnp.float32)
    o_h, s_fin = hgrn_chunked(fq, fk, flogf, fv, s0)
    o_hg = hgrn_output(o_h, hg, hg_norm_w)
    x = finish_layer(x, o_nsa, o_hg, o_moba, mg, w_branch, w_out, norm2_w, w_up, w_down)
    wb = min(WINDOW, t)
    return (x, kv[:, :, 0:2], kv[:, :, 2:4], jnp.stack([mk, mv], axis=2),
            kv[:, t - wb:, 4:6], s_fin.astype(x.dtype))


def sample_layer(x, pool_cmp, pool_slc, pool_moba, win_buf, s_prev, page_table, lb, lw):
    norm1_w, w_in, cmp_pos, cmp_w1, cmp_w2, hg_norm_w, w_branch, w_out, norm2_w, w_up, w_down = lw
    b, t, _ = x.shape
    past_len = page_table.shape[1] * PAGE_SIZE
    h = rmsnorm(x, norm1_w)
    nq, nkv, ngate, mqkv, hq, hf, hi, hg, mg = split_proj(h, w_in)
    pos = past_len + jnp.arange(t, dtype=jnp.int32)
    q = nq.reshape(b, t, NSA_H, HEAD_DIM)
    kv = nkv.reshape(b, t, 6, NSA_KVH, HEAD_DIM)

    def nsa_one(a):
        q_b, kv_b, pt = a
        past_c = gather_pages(pool_cmp, pt)
        past_s = gather_pages(pool_slc, pt)
        kc = jnp.concatenate([past_c[:, 0], kv_b[:, 0]], axis=0)
        vc = jnp.concatenate([past_c[:, 1], kv_b[:, 1]], axis=0)
        ks = jnp.concatenate([past_s[:, 0], kv_b[:, 2]], axis=0)
        vs = jnp.concatenate([past_s[:, 1], kv_b[:, 3]], axis=0)
        return nsa_seq(q_b, pos, kc, vc, ks, vs, cmp_pos, cmp_w1, cmp_w2)

    o_cmp, o_slc = lax.map(nsa_one, (q, kv, page_table))
    wb = win_buf.shape[1]
    win_all = jnp.concatenate([win_buf, kv[:, :, 4:6]], axis=1)
    k_pos = past_len - wb + jnp.arange(wb + t, dtype=jnp.int32)
    o_win = window_direct(q, pos, win_all[:, :, 0], win_all[:, :, 1], k_pos)
    o_nsa = nsa_combine(ngate, o_cmp, o_slc, o_win)
    mqkv = mqkv.reshape(b, t, 3, MOBA_H, HEAD_DIM)
    mq, mk, mv = mqkv[:, :, 0], mqkv[:, :, 1], mqkv[:, :, 2]

    def moba_one(a):
        q_b, k_b, v_b, pt = a
        past = gather_pages(pool_moba, pt)
        return moba_seq(q_b, pos, jnp.concatenate([past[:, 0], k_b], axis=0),
                        jnp.concatenate([past[:, 1], v_b], axis=0))

    o_moba = lax.map(moba_one, (mq, mk, mv, page_table)).reshape(b, t, MIX_W)
    fq, fk, flogf, fv = hgrn_features(hq, hf, hi, lb)
    o_h, s_new = hgrn_recurrent(fq, fk, flogf, fv, s_prev.astype(jnp.float32))
    o_hg = hgrn_output(o_h, hg, hg_norm_w)
    x = finish_layer(x, o_nsa, o_hg, o_moba, mg, w_branch, w_out, norm2_w, w_up, w_down)
    return (x, kv[:, :, 0:2], kv[:, :, 2:4], jnp.stack([mk, mv], axis=2),
            win_all[:, t:], s_new.astype(s_prev.dtype))


def setup_inputs(seed: int = 0) -> dict:
    key = jax.random.key(seed)
    ks = jax.random.split(key, 24)
    n_pages = PAST_LEN // PAGE_SIZE
    n_used = DEC_BATCH * n_pages
    n_phys = n_used + max(n_used // 4, 1)
    perm = jax.random.permutation(ks[0], n_phys)
    page_table = perm[:n_used].reshape(DEC_BATCH, n_pages).astype(jnp.int32)
    win_len = min(WINDOW, PAST_LEN)

    def nrm(k, shape, s=1.0):
        return s * jax.random.normal(k, shape, jnp.float32)

    return {
        'x_prompt': nrm(ks[1], (BATCH, SEQ, D_MODEL)),
        'x_sample': nrm(ks[2], (DEC_BATCH, DEC_SEQ, D_MODEL)),
        'cache_nsa_cmp': nrm(ks[3], (DEPTH, n_phys, PAGE_SIZE, 2, NSA_KVH, HEAD_DIM)),
        'cache_nsa_slc': nrm(ks[4], (DEPTH, n_phys, PAGE_SIZE, 2, NSA_KVH, HEAD_DIM)),
        'cache_moba': nrm(ks[5], (DEPTH, n_phys, PAGE_SIZE, 2, MOBA_H, HEAD_DIM)),
        'cache_nsa_win': nrm(ks[6], (DEPTH, DEC_BATCH, win_len, 2, NSA_KVH, HEAD_DIM)),
        'state_hgrn': nrm(ks[7], (DEPTH, DEC_BATCH, HG_H, HG_DK, HG_DV), 0.5),
        'page_table': page_table,
        'norm1_w': 1.0 + nrm(ks[8], (DEPTH, D_MODEL), 0.02),
        'norm2_w': 1.0 + nrm(ks[9], (DEPTH, D_MODEL), 0.02),
        'w_in': nrm(ks[10], (DEPTH, D_MODEL, D_IN), D_MODEL ** -0.5),
        'nsa_cmp_pos': nrm(ks[11], (DEPTH, 2, CMP_LEN, HEAD_DIM), 0.1),
        'nsa_cmp_w1': nrm(ks[12], (DEPTH, 2, CMP_LEN * HEAD_DIM, HEAD_DIM), (CMP_LEN * HEAD_DIM) ** -0.5),
        'nsa_cmp_w2': nrm(ks[13], (DEPTH, 2, HEAD_DIM, HEAD_DIM), HEAD_DIM ** -0.5),
        'hgrn_lb_logits': nrm(ks[14], (DEPTH, HG_H * HG_DK), 0.5),
        'hgrn_norm_w': 1.0 + nrm(ks[15], (DEPTH, MIX_W), 0.02),
        'w_branch': nrm(ks[16], (DEPTH, N_BRANCH, MIX_W, D_MODEL), MIX_W ** -0.5),
        'w_out': nrm(ks[17], (DEPTH, D_MODEL, D_MODEL), D_MODEL ** -0.5),
        'w_up': nrm(ks[18], (DEPTH, D_MODEL, D_FF), D_MODEL ** -0.5),
        'w_down': nrm(ks[19], (DEPTH, D_FF, D_MODEL), D_FF ** -0.5),
        'final_norm_w': 1.0 + nrm(ks[20], (D_MODEL,), 0.02),
    }


def reference(x_prompt, x_sample, cache_nsa_cmp, cache_nsa_slc, cache_moba, cache_nsa_win, state_hgrn,
              page_table, norm1_w, norm2_w, w_in, nsa_cmp_pos, nsa_cmp_w1, nsa_cmp_w2, hgrn_lb_logits,
              hgrn_norm_w, w_branch, w_out, w_up, w_down, final_norm_w):
    sm = jax.nn.softmax(hgrn_lb_logits.astype(jnp.float32), axis=0)
    lbs = jnp.cumsum(sm, axis=0) - sm[0:1]
    xp, xs = x_prompt, x_sample
    outs_p, outs_s = [], []
    for i in range(DEPTH):
        lw = (norm1_w[i], w_in[i], nsa_cmp_pos[i], nsa_cmp_w1[i], nsa_cmp_w2[i], hgrn_norm_w[i],
              w_branch[i], w_out[i], norm2_w[i], w_up[i], w_down[i])
        xp, *st_p = prompt_layer(xp, lbs[i], lw)
        xs, *st_s = sample_layer(xs, cache_nsa_cmp[i], cache_nsa_slc[i], cache_moba[i], cache_nsa_win[i],
                                 state_hgrn[i], page_table, lbs[i], lw)
        outs_p.append(st_p)
        outs_s.append(st_s)
    y_prompt = rmsnorm(xp, final_norm_w)
    y_sample = rmsnorm(xs, final_norm_w)
    p_cmp = jnp.stack([o[0] for o in outs_p])
    p_slc = jnp.stack([o[1] for o in outs_p])
    p_moba = jnp.stack([o[2] for o in outs_p])
    p_win = jnp.stack([o[3] for o in outs_p])
    p_hg = jnp.stack([o[4] for o in outs_p])
    s_cmp = jnp.stack([o[0] for o in outs_s])
    s_slc = jnp.stack([o[1] for o in outs_s])
    s_moba = jnp.stack([o[2] for o in outs_s])
    s_win = jnp.stack([o[3] for o in outs_s])
    s_hg = jnp.stack([o[4] for o in outs_s])
    return (y_prompt, y_sample, p_cmp, p_slc, p_moba, p_win, p_hg, s_cmp, s_slc, s_moba, s_win, s_hg)
```

```python
import functools
import math

import jax
import jax.numpy as jnp
import numpy as np
from jax import lax
from jax.experimental import pallas as pl
from jax.experimental.pallas import tpu as pltpu

D_MODEL = 2048
DEPTH = 2
PAGE_SIZE = 128
HEAD_DIM = 128
MIX_W = D_MODEL // 2
N_BRANCH = 3
NSA_H = MIX_W // HEAD_DIM
NSA_KVH = NSA_H // 4
CMP_LEN = 32
CMP_STRIDE = 16
SEL_BLOCK = 64
SEL_TOPK = 16
WINDOW = 512
WIN_QBLOCK = 128
FORCE_BONUS = 1e4
MOBA_H = MIX_W // HEAD_DIM
MOBA_BLOCK = 256
MOBA_TOPK = 3
HG_H = MIX_W // HEAD_DIM
HG_DK = 128
HG_DV = MIX_W // HG_H
HG_CHUNK = 64
D_FF = 4 * D_MODEL
Q_CHUNK = 32
EPS = 1e-6
F_FLOOR = 1e-30
NEG = -1e30

N_NQ = NSA_H * HEAD_DIM
N_NKV = 6 * NSA_KVH * HEAD_DIM
N_NGATE = 3 * NSA_H
N_MQKV = 3 * MOBA_H * HEAD_DIM
N_HG = HG_H * HG_DK
N_MG = N_BRANCH * D_MODEL
OFF_NQ = 0
OFF_NKV = OFF_NQ + N_NQ
OFF_MQKV = OFF_NKV + N_NKV
OFF_HQ = OFF_MQKV + N_MQKV
OFF_HF = OFF_HQ + N_HG
OFF_HI = OFF_HF + N_HG
OFF_HGATE = OFF_HI + N_HG
OFF_MG = OFF_HGATE + N_HG
OFF_NGATE = OFF_MG + N_MG
D_IN_PAD = 16384

VMEM_LIMIT_BYTES = 48 * 1024 * 1024


def _cparams(sem):
    return pltpu.CompilerParams(dimension_semantics=sem, vmem_limit_bytes=VMEM_LIMIT_BYTES)


def _rmsnorm_kernel(x_ref, w_ref, o_ref):
    x = x_ref[...]
    y = x * lax.rsqrt(jnp.mean(x * x, axis=-1, keepdims=True) + EPS)
    o_ref[...] = (y * w_ref[...]).astype(o_ref.dtype)


def rmsnorm_rows(x, w, out_dtype):
    m, d = x.shape
    tm = min(m, 512)
    return pl.pallas_call(
        _rmsnorm_kernel,
        grid=(m // tm,),
        in_specs=[pl.BlockSpec((tm, d), lambda i: (i, 0)), pl.BlockSpec((1, d), lambda i: (0, 0))],
        out_specs=pl.BlockSpec((tm, d), lambda i: (i, 0)),
        out_shape=jax.ShapeDtypeStruct((m, d), out_dtype),
        compiler_params=_cparams(("parallel",)),
        name="rmsnorm",
    )(x, w.reshape(1, d))


def _mm_kernel(*refs, epilogue, nk):
    if epilogue == "residual":
        a_ref, w_ref, r_ref, o_ref = refs[:4]
        rest = refs[4:]
    else:
        a_ref, w_ref, o_ref = refs[:3]
        r_ref = None
        rest = refs[3:]
    part = jnp.dot(a_ref[...], w_ref[...], preferred_element_type=jnp.float32)

    def finish(acc):
        if epilogue == "residual":
            o_ref[...] = r_ref[...] + acc
        elif epilogue == "relu2":
            u = jnp.maximum(acc, 0.0)
            o_ref[...] = (u * u).astype(o_ref.dtype)
        else:
            o_ref[...] = acc.astype(o_ref.dtype)

    if nk == 1:
        finish(part)
    else:
        acc_ref = rest[0]
        k = pl.program_id(2)

        @pl.when(k == 0)
        def _():
            acc_ref[...] = part

        @pl.when(k > 0)
        def _():
            acc_ref[...] += part

        @pl.when(k == nk - 1)
        def _():
            finish(acc_ref[...])


def matmul(a, w, *, epilogue="none", residual=None, out_dtype=jnp.float32, tm=1024, tn=1024, tk=2048):
    m, kdim = a.shape
    n = w.shape[1]
    tm, tn, tk = min(tm, m), min(tn, n), min(tk, kdim)
    nk = kdim // tk
    in_specs = [pl.BlockSpec((tm, tk), lambda j, i, k: (i, k)),
                pl.BlockSpec((tk, tn), lambda j, i, k: (k, j))]
    args = [a, w]
    if epilogue == "residual":
        in_specs.append(pl.BlockSpec((tm, tn), lambda j, i, k: (i, j)))
        args.append(residual)
    scratch = [pltpu.VMEM((tm, tn), jnp.float32)] if nk > 1 else []
    return pl.pallas_call(
        functools.partial(_mm_kernel, epilogue=epilogue, nk=nk),
        grid=(n // tn, m // tm, nk),
        in_specs=in_specs,
        out_specs=pl.BlockSpec((tm, tn), lambda j, i, k: (i, j)),
        out_shape=jax.ShapeDtypeStruct((m, n), out_dtype),
        scratch_shapes=scratch,
        compiler_params=_cparams(("parallel", "parallel", "arbitrary")),
        name="matmul_" + epilogue,
    )(*args)


def _merge_kernel(br_ref, wb_ref, g0_ref, g1_ref, g2_ref, o_ref):
    acc = None
    for n, g_ref in enumerate((g0_ref, g1_ref, g2_ref)):
        proj = jnp.dot(br_ref[:, n * MIX_W:(n + 1) * MIX_W], wb_ref[n], preferred_element_type=jnp.float32)
        term = jax.nn.sigmoid(g_ref[...]) * proj
        acc = term if acc is None else acc + term
    o_ref[...] = acc.astype(o_ref.dtype)


def branch_merge(br, wb, z, *, tm=512, tn=512):
    m = br.shape[0]
    tm = min(tm, m)
    gate_specs = [
        pl.BlockSpec((tm, tn), lambda j, i, n=n: (i, (OFF_MG + n * D_MODEL) // tn + j)) for n in range(N_BRANCH)]
    return pl.pallas_call(
        _merge_kernel,
        grid=(D_MODEL // tn, m // tm),
        in_specs=[pl.BlockSpec((tm, N_BRANCH * MIX_W), lambda j, i: (i, 0)),
                  pl.BlockSpec((N_BRANCH, MIX_W, tn), lambda j, i: (0, 0, j))] + gate_specs,
        out_specs=pl.BlockSpec((tm, tn), lambda j, i: (i, j)),
        out_shape=jax.ShapeDtypeStruct((m, D_MODEL), jnp.bfloat16),
        compiler_params=_cparams(("parallel", "parallel")),
        name="branch_merge",
    )(br, wb, z, z, z)


def masked_probs(s, mask):
    p = jax.nn.softmax(jnp.where(mask, s, NEG), axis=-1)
    return jnp.where(mask, p, 0.0)


def map_query_chunks(fn, *xs):
    n_q = xs[0].shape[0]
    qc = math.gcd(n_q, Q_CHUNK)
    n = n_q // qc
    blocks = tuple(a.reshape((n, qc) + a.shape[1:]) for a in xs)
    out = lax.map(lambda a: fn(*a), blocks)
    return out.reshape((n_q,) + out.shape[2:])


def gather_pages(pool, pages):
    rows = pool[pages]
    return rows.reshape((-1,) + pool.shape[2:])


def nsa_compress(x, pos_emb, w1, w2):
    n_blk = (x.shape[0] - CMP_LEN) // CMP_STRIDE + 1
    idx = np.arange(n_blk)[:, None] * CMP_STRIDE + np.arange(CMP_LEN)[None, :]
    blk = x[idx] + pos_emb[None, :, None, :]
    blk = blk.transpose(0, 2, 1, 3).reshape(n_blk, x.shape[1], CMP_LEN * HEAD_DIM)
    return jax.nn.silu(blk @ w1) @ w2


def nsa_seq(q, q_pos, kc, vc, ks, vs, cmp_pos, cmp_w1, cmp_w2):
    n_q, seq_len = q.shape[0], kc.shape[0]
    rep = NSA_H // NSA_KVH
    scale = HEAD_DIM ** -0.5
    qg = q.reshape(n_q, NSA_KVH, rep, HEAD_DIM)
    k_cmp = nsa_compress(kc, cmp_pos[0], cmp_w1[0], cmp_w2[0])
    v_cmp = nsa_compress(vc, cmp_pos[1], cmp_w1[1], cmp_w2[1])
    n_cmp = k_cmp.shape[0]
    cmp_start = np.arange(n_cmp, dtype=np.int32) * CMP_STRIDE
    s = jnp.einsum('qgrd,ngd->qgrn', qg, k_cmp, preferred_element_type=jnp.float32) * scale
    cmask = (cmp_start + CMP_LEN - 1)[None, :] <= q_pos[:, None]
    p_cmp = masked_probs(s, cmask[:, None, None, :])
    o_cmp = jnp.einsum('qgrn,ngd->qgrd', p_cmp.astype(v_cmp.dtype), v_cmp)
    n_sel = -(-seq_len // SEL_BLOCK)
    sel_start = np.arange(n_sel, dtype=np.int32) * SEL_BLOCK
    overlap = ((cmp_start[:, None] < sel_start[None, :] + SEL_BLOCK)
               & (cmp_start[:, None] + CMP_LEN > sel_start[None, :])).astype(np.float32)
    imp = jnp.einsum('qgrn,nj->qgj', p_cmp, overlap, precision=lax.Precision.HIGHEST)
    cur = q_pos // SEL_BLOCK
    blk = np.arange(n_sel, dtype=np.int32)[None, :]
    valid = blk <= cur[:, None]
    forced = (blk == 0) | (blk == cur[:, None]) | (blk == cur[:, None] - 1)
    score = jnp.where(valid[:, None, :],
                      imp + jnp.where(forced, FORCE_BONUS, 0.0)[:, None, :], NEG)
    n_top = min(SEL_TOPK, n_sel)
    _, idx = lax.top_k(score, n_top)
    ok = np.arange(n_top, dtype=np.int32)[None, :] < jnp.minimum(cur + 1, n_top)[:, None]
    pad = n_sel * SEL_BLOCK - seq_len
    ksb = jnp.pad(ks, ((0, pad), (0, 0), (0, 0))).reshape(
        n_sel, SEL_BLOCK, NSA_KVH, HEAD_DIM).transpose(2, 0, 1, 3)
    vsb = jnp.pad(vs, ((0, pad), (0, 0), (0, 0))).reshape(
        n_sel, SEL_BLOCK, NSA_KVH, HEAD_DIM).transpose(2, 0, 1, 3)
    g_ix = np.arange(NSA_KVH)[None, :, None]
    offs = np.arange(SEL_BLOCK, dtype=np.int32)

    def sel_chunk(q_c, pos_c, idx_c, ok_c):
        n = q_c.shape[0]
        k_g = ksb[g_ix, idx_c]
        v_g = vsb[g_ix, idx_c]
        k_pos = idx_c[..., None] * SEL_BLOCK + offs
        m = (k_pos <= pos_c[:, None, None, None]) & ok_c[:, None, :, None]
        s_c = jnp.einsum('qgrd,qgkbd->qgrkb', q_c, k_g, preferred_element_type=jnp.float32) * scale
        p_c = masked_probs(s_c.reshape(n, NSA_KVH, rep, -1), m.reshape(n, NSA_KVH, 1, -1))
        return jnp.einsum('qgrkb,qgkbd->qgrd', p_c.reshape(s_c.shape).astype(v_g.dtype), v_g)

    o_slc = map_query_chunks(sel_chunk, qg, q_pos, idx, ok)
    return o_cmp.reshape(n_q, NSA_H, HEAD_DIM), o_slc.reshape(n_q, NSA_H, HEAD_DIM)


def window_banded(q, k, v):
    b, t = q.shape[:2]
    rep = NSA_H // NSA_KVH
    qb = math.gcd(t, WIN_QBLOCK)
    nb = t // qb
    span = WINDOW + qb
    idx = np.arange(nb)[:, None] * qb + np.arange(span)[None, :]
    k_pos = idx - WINDOW
    q_pos = np.arange(t).reshape(nb, qb)
    d = q_pos[:, :, None] - k_pos[:, None, :]
    mask = (d >= 0) & (d <= WINDOW) & (k_pos[:, None, :] >= 0)
    kp = jnp.pad(k, ((0, 0), (WINDOW, 0), (0, 0), (0, 0)))[:, idx]
    vp = jnp.pad(v, ((0, 0), (WINDOW, 0), (0, 0), (0, 0)))[:, idx]
    qg = q.reshape(b, nb, qb, NSA_KVH, rep, HEAD_DIM)
    s = jnp.einsum('bnqgrd,bnkgd->bnqgrk', qg, kp, preferred_element_type=jnp.float32) * (HEAD_DIM ** -0.5)
    p = masked_probs(s, mask[None, :, :, None, None, :])
    o = jnp.einsum('bnqgrk,bnkgd->bnqgrd', p.astype(vp.dtype), vp)
    return o.reshape(b, t, NSA_H, HEAD_DIM)


def window_direct(q, q_pos, k, v, k_pos):
    b, n_q = q.shape[:2]
    rep = NSA_H // NSA_KVH
    qg = q.reshape(b, n_q, NSA_KVH, rep, HEAD_DIM)
    s = jnp.einsum('bqgrd,bkgd->bqgrk', qg, k, preferred_element_type=jnp.float32) * (HEAD_DIM ** -0.5)
    d = q_pos[:, None] - k_pos[None, :]
    mask = (d >= 0) & (d <= WINDOW)
    p = masked_probs(s, mask[None, :, None, None, :])
    o = jnp.einsum('bqgrk,bkgd->bqgrd', p.astype(v.dtype), v)
    return o.reshape(b, n_q, NSA_H, HEAD_DIM)


def nsa_combine(gate_logits, o_cmp, o_slc, o_win):
    b, t = o_cmp.shape[:2]
    g = jax.nn.sigmoid(gate_logits.astype(jnp.float32)).reshape(b, t, 3, NSA_H, 1)
    o = g[:, :, 0] * o_cmp + g[:, :, 1] * o_slc + g[:, :, 2] * o_win
    return o.reshape(b, t, MIX_W).astype(o_cmp.dtype)


def moba_seq(q, q_pos, k, v):
    n_q, seq_len = q.shape[0], k.shape[0]
    scale = HEAD_DIM ** -0.5
    nb = -(-seq_len // MOBA_BLOCK)
    pad = nb * MOBA_BLOCK - seq_len
    kb = jnp.pad(k, ((0, pad), (0, 0), (0, 0))).reshape(nb, MOBA_BLOCK, MOBA_H, HEAD_DIM)
    vb = jnp.pad(v, ((0, pad), (0, 0), (0, 0))).reshape(nb, MOBA_BLOCK, MOBA_H, HEAD_DIM)
    k_mean = jnp.mean(kb.astype(jnp.float32), axis=1)
    cur = q_pos // MOBA_BLOCK
    gate = jnp.einsum('qhd,nhd->qhn', q.astype(jnp.float32), k_mean, precision=lax.Precision.HIGHEST)
    past = np.arange(nb, dtype=np.int32)[None, :] < cur[:, None]
    gate = jnp.where(past[:, None, :], gate, NEG)
    n_top = min(MOBA_TOPK, nb)
    _, idx = lax.top_k(gate, n_top)
    ok = np.arange(n_top, dtype=np.int32)[None, :] < jnp.minimum(cur, n_top)[:, None]
    idx = jnp.concatenate([idx, jnp.broadcast_to(cur[:, None, None], (n_q, MOBA_H, 1))], axis=-1)
    ok = jnp.concatenate([ok, jnp.ones((n_q, 1), dtype=bool)], axis=-1)
    kbh = kb.transpose(2, 0, 1, 3)
    vbh = vb.transpose(2, 0, 1, 3)
    h_ix = np.arange(MOBA_H)[None, :, None]
    offs = np.arange(MOBA_BLOCK, dtype=np.int32)

    def chunk(q_c, pos_c, idx_c, ok_c):
        n = q_c.shape[0]
        k_g = kbh[h_ix, idx_c]
        v_g = vbh[h_ix, idx_c]
        k_pos = idx_c[..., None] * MOBA_BLOCK + offs
        m = (k_pos <= pos_c[:, None, None, None]) & ok_c[:, None, :, None]
        s_c = jnp.einsum('qhd,qhkbd->qhkb', q_c, k_g, preferred_element_type=jnp.float32) * scale
        p_c = masked_probs(s_c.reshape(n, MOBA_H, -1), m.reshape(n, MOBA_H, -1))
        return jnp.einsum('qhkb,qhkbd->qhd', p_c.reshape(s_c.shape).astype(v_g.dtype), v_g)

    return map_query_chunks(chunk, q, q_pos, idx, ok)


def hgrn_features(hq, hf, hi, lb):
    b, t, _ = hq.shape
    q = jax.nn.silu(hq.astype(jnp.float32)).reshape(b, t, HG_H, HG_DK)
    z = hf.astype(jnp.float32).reshape(b, t, HG_H, HG_DK)
    lbh = lb.astype(jnp.float32).reshape(HG_H, HG_DK)
    f = lbh + (1.0 - lbh) * jax.nn.sigmoid(z)
    logf = jnp.log(jnp.maximum(f, F_FLOOR))
    k = (1.0 - lbh) * jax.nn.sigmoid(-z)
    v = hi.astype(jnp.float32).reshape(b, t, HG_H, HG_DV)
    return q, k, logf, v


def hgrn_chunked(q, k, logf, v, s0):
    b, t, h, _ = q.shape
    c = math.gcd(t, HG_CHUNK)
    n = t // c
    tri = np.tril(np.ones((c, c), dtype=bool))[:, :, None]

    def blocks(a):
        return a.reshape(b, n, c, h, a.shape[-1]).transpose(1, 0, 3, 2, 4)

    def step(s, inp):
        q_c, k_c, l_c, v_c = inp
        cum = jnp.cumsum(l_c, axis=2)
        diff = cum[:, :, :, None, :] - cum[:, :, None, :, :]
        decay = jnp.where(tri, jnp.exp(jnp.where(tri, diff, 0.0)), 0.0)
        a = jnp.einsum('bhtd,bhsd,bhtsd->bhts', q_c, k_c, decay)
        o = jnp.einsum('bhts,bhse->bhte', a, v_c) + jnp.einsum('bhtd,bhde->bhte', q_c * jnp.exp(cum), s)
        last = cum[:, :, -1]
        s = jnp.exp(last)[..., None] * s + jnp.einsum(
            'bhsd,bhse->bhde', k_c * jnp.exp(last[:, :, None] - cum), v_c)
        return s, o

    s, o = lax.scan(step, s0, (blocks(q), blocks(k), blocks(logf), blocks(v)))
    return o.transpose(1, 0, 3, 2, 4).reshape(b, t, h, v.shape[-1]), s


def hgrn_recurrent(q, k, logf, v, s0):
    def step(s, inp):
        q_t, k_t, l_t, v_t = inp
        s = jnp.exp(l_t)[..., None] * s + k_t[..., None] * v_t[..., None, :]
        return s, jnp.einsum('bhd,bhde->bhe', q_t, s)

    sw = lambda a: jnp.swapaxes(a, 0, 1)
    s, o = lax.scan(step, s0, (sw(q), sw(k), sw(logf), sw(v)))
    return sw(o), s


def hgrn_output(o, hg, norm_w):
    b, t = o.shape[:2]
    o = o * lax.rsqrt(jnp.mean(o * o, axis=-1, keepdims=True) + EPS)
    o = o * norm_w.astype(jnp.float32).reshape(HG_H, HG_DV)
    g = jax.nn.silu(hg.astype(jnp.float32)).reshape(b, t, HG_H, HG_DV)
    return (o * g).reshape(b, t, MIX_W).astype(hg.dtype)


def split_z(z, b, t):
    z = z.reshape(b, t, D_IN_PAD)
    sl = lambda off, n: z[:, :, off:off + n]
    return (sl(OFF_NQ, N_NQ), sl(OFF_NKV, N_NKV), sl(OFF_NGATE, N_NGATE), sl(OFF_MQKV, N_MQKV),
            sl(OFF_HQ, N_HG), sl(OFF_HF, N_HG), sl(OFF_HI, N_HG), sl(OFF_HGATE, N_HG))


def finish_layer(x2, z, o_nsa, o_hg, o_moba, lw):
    br = jnp.concatenate([o_nsa, o_hg, o_moba], axis=-1).astype(jnp.bfloat16)
    mixed = branch_merge(br, lw["w_branch"], z)
    x2 = matmul(mixed, lw["w_out"], epilogue="residual", residual=x2)
    h2 = rmsnorm_rows(x2, lw["norm2_w"], jnp.bfloat16)
    u2 = matmul(h2, lw["w_up"], epilogue="relu2", out_dtype=jnp.bfloat16)
    return matmul(u2, lw["w_down"], epilogue="residual", residual=x2)


def prompt_layer(x, lb, lw):
    b, t, _ = x.shape
    x2 = x.reshape(b * t, D_MODEL)
    h = rmsnorm_rows(x2, lw["norm1_w"], jnp.bfloat16)
    z = matmul(h, lw["w_in"])
    nq, nkv, ngate, mqkv, hq, hf, hi, hg = split_z(z, b, t)
    pos = jnp.arange(t, dtype=jnp.int32)
    q = nq.reshape(b, t, NSA_H, HEAD_DIM)
    kv = nkv.reshape(b, t, 6, NSA_KVH, HEAD_DIM)
    o_cmp, o_slc = lax.map(lambda a: nsa_seq(a[0], pos, a[1][:, 0], a[1][:, 1], a[1][:, 2], a[1][:, 3],
                                             lw["cmp_pos"], lw["cmp_w1"], lw["cmp_w2"]), (q, kv))
    o_win = window_banded(q, kv[:, :, 4], kv[:, :, 5])
    o_nsa = nsa_combine(ngate, o_cmp, o_slc, o_win)
    mqkv = mqkv.reshape(b, t, 3, MOBA_H, HEAD_DIM)
    mq, mk, mv = mqkv[:, :, 0], mqkv[:, :, 1], mqkv[:, :, 2]
    o_moba = lax.map(lambda a: moba_seq(a[0], pos, a[1], a[2]), (mq, mk, mv)).reshape(b, t, MIX_W)
    fq, fk, flogf, fv = hgrn_features(hq, hf, hi, lb)
    s0 = jnp.zeros((b, HG_H, HG_DK, HG_DV), jnp.float32)
    o_h, s_fin = hgrn_chunked(fq, fk, flogf, fv, s0)
    o_hg = hgrn_output(o_h, hg, lw["hg_norm_w"])
    m = b * t
    x2 = finish_layer(x2, z, o_nsa.reshape(m, MIX_W), o_hg.reshape(m, MIX_W), o_moba.reshape(m, MIX_W), lw)
    wb = min(WINDOW, t)
    return (x2.reshape(b, t, D_MODEL), kv[:, :, 0:2], kv[:, :, 2:4], jnp.stack([mk, mv], axis=2),
            kv[:, t - wb:, 4:6], s_fin)


def sample_layer(x, pool_cmp, pool_slc, pool_moba, win_buf, s_prev, page_table, lb, lw):
    b, t, _ = x.shape
    past_len = page_table.shape[1] * PAGE_SIZE
    x2 = x.reshape(b * t, D_MODEL)
    h = rmsnorm_rows(x2, lw["norm1_w"], jnp.bfloat16)
    z = matmul(h, lw["w_in"])
    nq, nkv, ngate, mqkv, hq, hf, hi, hg = split_z(z, b, t)
    pos = past_len + jnp.arange(t, dtype=jnp.int32)
    q = nq.reshape(b, t, NSA_H, HEAD_DIM)
    kv = nkv.reshape(b, t, 6, NSA_KVH, HEAD_DIM)

    def nsa_one(a):
        q_b, kv_b, pt = a
        past_c = gather_pages(pool_cmp, pt)
        past_s = gather_pages(pool_slc, pt)
        kc = jnp.concatenate([past_c[:, 0], kv_b[:, 0]], axis=0)
        vc = jnp.concatenate([past_c[:, 1], kv_b[:, 1]], axis=0)
        ks = jnp.concatenate([past_s[:, 0], kv_b[:, 2]], axis=0)
        vs = jnp.concatenate([past_s[:, 1], kv_b[:, 3]], axis=0)
        return nsa_seq(q_b, pos, kc, vc, ks, vs, lw["cmp_pos"], lw["cmp_w1"], lw["cmp_w2"])

    o_cmp, o_slc = lax.map(nsa_one, (q, kv, page_table))
    wb = win_buf.shape[1]
    win_all = jnp.concatenate([win_buf, kv[:, :, 4:6]], axis=1)
    k_pos = past_len - wb + jnp.arange(wb + t, dtype=jnp.int32)
    o_win = window_direct(q, pos, win_all[:, :, 0], win_all[:, :, 1], k_pos)
    o_nsa = nsa_combine(ngate, o_cmp, o_slc, o_win)
    mqkv = mqkv.reshape(b, t, 3, MOBA_H, HEAD_DIM)
    mq, mk, mv = mqkv[:, :, 0], mqkv[:, :, 1], mqkv[:, :, 2]

    def moba_one(a):
        q_b, k_b, v_b, pt = a
        past = gather_pages(pool_moba, pt)
        return moba_seq(q_b, pos, jnp.concatenate([past[:, 0], k_b], axis=0),
                        jnp.concatenate([past[:, 1], v_b], axis=0))

    o_moba = lax.map(moba_one, (mq, mk, mv, page_table)).reshape(b, t, MIX_W)
    fq, fk, flogf, fv = hgrn_features(hq, hf, hi, lb)
    o_h, s_new = hgrn_recurrent(fq, fk, flogf, fv, s_prev.astype(jnp.float32))
    o_hg = hgrn_output(o_h, hg, lw["hg_norm_w"])
    m = b * t
    x2 = finish_layer(x2, z, o_nsa.reshape(m, MIX_W), o_hg.reshape(m, MIX_W), o_moba.reshape(m, MIX_W), lw)
    return (x2.reshape(b, t, D_MODEL), kv[:, :, 0:2], kv[:, :, 2:4], jnp.stack([mk, mv], axis=2),
            win_all[:, t:], s_new)


def prep_layer_weights(i, norm1_w, norm2_w, w_in, nsa_cmp_pos, nsa_cmp_w1, nsa_cmp_w2, hgrn_norm_w,
                       w_branch, w_out, w_up, w_down):
    wi = w_in[i]
    gate_lo = N_NQ + N_NKV
    gate_hi = gate_lo + N_NGATE
    pad = jnp.zeros((D_MODEL, D_IN_PAD - wi.shape[1]), wi.dtype)
    wi = jnp.concatenate([wi[:, :gate_lo], wi[:, gate_hi:], wi[:, gate_lo:gate_hi], pad], axis=1)
    bf = lambda a: a.astype(jnp.bfloat16)
    return dict(norm1_w=norm1_w[i], norm2_w=norm2_w[i], w_in=bf(wi), cmp_pos=nsa_cmp_pos[i],
                cmp_w1=nsa_cmp_w1[i], cmp_w2=nsa_cmp_w2[i], hg_norm_w=hgrn_norm_w[i],
                w_branch=bf(w_branch[i]), w_out=bf(w_out[i]), w_up=bf(w_up[i]), w_down=bf(w_down[i]))


def kernel(x_prompt, x_sample, cache_nsa_cmp, cache_nsa_slc, cache_moba, cache_nsa_win, state_hgrn,
           page_table, norm1_w, norm2_w, w_in, nsa_cmp_pos, nsa_cmp_w1, nsa_cmp_w2, hgrn_lb_logits,
           hgrn_norm_w, w_branch, w_out, w_up, w_down, final_norm_w):
    sm = jax.nn.softmax(hgrn_lb_logits.astype(jnp.float32), axis=0)
    lbs = jnp.cumsum(sm, axis=0) - sm[0:1]
    xp, xs = x_prompt, x_sample
    outs_p, outs_s = [], []
    for i in range(DEPTH):
        lw = prep_layer_weights(i, norm1_w, norm2_w, w_in, nsa_cmp_pos, nsa_cmp_w1, nsa_cmp_w2,
                                hgrn_norm_w, w_branch, w_out, w_up, w_down)
        xp, *st_p = prompt_layer(xp, lbs[i], lw)
        xs, *st_s = sample_layer(xs, cache_nsa_cmp[i], cache_nsa_slc[i], cache_moba[i], cache_nsa_win[i],
                                 state_hgrn[i], page_table, lbs[i], lw)
        outs_p.append(st_p)
        outs_s.append(st_s)
    y_prompt = rmsnorm_rows(xp.reshape(-1, D_MODEL), final_norm_w, jnp.float32).reshape(xp.shape)
    y_sample = rmsnorm_rows(xs.reshape(-1, D_MODEL), final_norm_w, jnp.float32).reshape(xs.shape)
    stack = lambda outs, j: jnp.stack([o[j] for o in outs])
    return (y_prompt, y_sample) + tuple(stack(outs_p, j) for j in range(5)) + tuple(
        stack(outs_s, j) for j in range(5))
```

```python
import functools
import math

import jax
import jax.numpy as jnp
import numpy as np
from jax import lax
from jax.experimental import pallas as pl
from jax.experimental.pallas import tpu as pltpu

D_MODEL = 2048
DEPTH = 2
PAGE_SIZE = 128
HEAD_DIM = 128
MIX_W = D_MODEL // 2
N_BRANCH = 3
NSA_H = MIX_W // HEAD_DIM
NSA_KVH = NSA_H // 4
CMP_LEN = 32
CMP_STRIDE = 16
SEL_BLOCK = 64
SEL_TOPK = 16
WINDOW = 512
WIN_QBLOCK = 128
FORCE_BONUS = 1e4
MOBA_H = MIX_W // HEAD_DIM
MOBA_BLOCK = 256
MOBA_TOPK = 3
HG_H = MIX_W // HEAD_DIM
HG_DK = 128
HG_DV = MIX_W // HG_H
HG_CHUNK = 64
D_FF = 4 * D_MODEL
Q_CHUNK = 32
EPS = 1e-6
F_FLOOR = 1e-30
NEG = -1e30

N_NQ = NSA_H * HEAD_DIM
N_NKV = 6 * NSA_KVH * HEAD_DIM
N_NGATE = 3 * NSA_H
N_MQKV = 3 * MOBA_H * HEAD_DIM
N_HG = HG_H * HG_DK
N_MG = N_BRANCH * D_MODEL
OFF_NQ = 0
OFF_NKV = OFF_NQ + N_NQ
OFF_MQKV = OFF_NKV + N_NKV
OFF_HQ = OFF_MQKV + N_MQKV
OFF_HF = OFF_HQ + N_HG
OFF_HI = OFF_HF + N_HG
OFF_HGATE = OFF_HI + N_HG
OFF_MG = OFF_HGATE + N_HG
OFF_NGATE = OFF_MG + N_MG
D_IN_PAD = 16384

VMEM_LIMIT_BYTES = 48 * 1024 * 1024


def _cparams(sem):
    return pltpu.CompilerParams(dimension_semantics=sem, vmem_limit_bytes=VMEM_LIMIT_BYTES)


def _rmsnorm_kernel(x_ref, w_ref, o_ref):
    x = x_ref[...]
    y = x * lax.rsqrt(jnp.mean(x * x, axis=-1, keepdims=True) + EPS)
    o_ref[...] = (y * w_ref[...]).astype(o_ref.dtype)


def rmsnorm_rows(x, w, out_dtype):
    m, d = x.shape
    tm = min(m, 512)
    return pl.pallas_call(
        _rmsnorm_kernel,
        grid=(m // tm,),
        in_specs=[pl.BlockSpec((tm, d), lambda i: (i, 0)), pl.BlockSpec((1, d), lambda i: (0, 0))],
        out_specs=pl.BlockSpec((tm, d), lambda i: (i, 0)),
        out_shape=jax.ShapeDtypeStruct((m, d), out_dtype),
        compiler_params=_cparams(("parallel",)),
        name="rmsnorm",
    )(x, w.reshape(1, d))


def _mm_kernel(*refs, epilogue, nk):
    if epilogue == "residual":
        a_ref, w_ref, r_ref, o_ref = refs[:4]
        rest = refs[4:]
    else:
        a_ref, w_ref, o_ref = refs[:3]
        r_ref = None
        rest = refs[3:]
    part = jnp.dot(a_ref[...], w_ref[...], preferred_element_type=jnp.float32)

    def finish(acc):
        if epilogue == "residual":
            o_ref[...] = r_ref[...] + acc
        elif epilogue == "relu2":
            u = jnp.maximum(acc, 0.0)
            o_ref[...] = (u * u).astype(o_ref.dtype)
        else:
            o_ref[...] = acc.astype(o_ref.dtype)

    if nk == 1:
        finish(part)
    else:
        acc_ref = rest[0]
        k = pl.program_id(2)

        @pl.when(k == 0)
        def _():
            acc_ref[...] = part

        @pl.when(k > 0)
        def _():
            acc_ref[...] += part

        @pl.when(k == nk - 1)
        def _():
            finish(acc_ref[...])


def matmul(a, w, *, epilogue="none", residual=None, out_dtype=jnp.float32, tm=1024, tn=1024, tk=2048):
    m, kdim = a.shape
    n = w.shape[1]
    tm, tn, tk = min(tm, m), min(tn, n), min(tk, kdim)
    nk = kdim // tk
    in_specs = [pl.BlockSpec((tm, tk), lambda j, i, k: (i, k)),
                pl.BlockSpec((tk, tn), lambda j, i, k: (k, j))]
    args = [a, w]
    if epilogue == "residual":
        in_specs.append(pl.BlockSpec((tm, tn), lambda j, i, k: (i, j)))
        args.append(residual)
    scratch = [pltpu.VMEM((tm, tn), jnp.float32)] if nk > 1 else []
    return pl.pallas_call(
        functools.partial(_mm_kernel, epilogue=epilogue, nk=nk),
        grid=(n // tn, m // tm, nk),
        in_specs=in_specs,
        out_specs=pl.BlockSpec((tm, tn), lambda j, i, k: (i, j)),
        out_shape=jax.ShapeDtypeStruct((m, n), out_dtype),
        scratch_shapes=scratch,
        compiler_params=_cparams(("parallel", "parallel", "arbitrary")),
        name="matmul_" + epilogue,
    )(*args)


def _merge_kernel(b0_ref, b1_ref, b2_ref, wb_ref, g0_ref, g1_ref, g2_ref, o_ref):
    acc = None
    for n, (b_ref, g_ref) in enumerate(((b0_ref, g0_ref), (b1_ref, g1_ref), (b2_ref, g2_ref))):
        proj = jnp.dot(b_ref[...], wb_ref[n], preferred_element_type=jnp.float32)
        term = jax.nn.sigmoid(g_ref[...]) * proj
        acc = term if acc is None else acc + term
    o_ref[...] = acc.astype(o_ref.dtype)


def branch_merge(branches, wb, z, *, tm=512, tn=512):
    m = branches[0].shape[0]
    tm = min(tm, m)
    gate_specs = [
        pl.BlockSpec((tm, tn), lambda j, i, n=n: (i, (OFF_MG + n * D_MODEL) // tn + j)) for n in range(N_BRANCH)]
    return pl.pallas_call(
        _merge_kernel,
        grid=(D_MODEL // tn, m // tm),
        in_specs=[pl.BlockSpec((tm, MIX_W), lambda j, i: (i, 0))] * N_BRANCH
        + [pl.BlockSpec((N_BRANCH, MIX_W, tn), lambda j, i: (0, 0, j))] + gate_specs,
        out_specs=pl.BlockSpec((tm, tn), lambda j, i: (i, j)),
        out_shape=jax.ShapeDtypeStruct((m, D_MODEL), jnp.bfloat16),
        compiler_params=_cparams(("parallel", "parallel")),
        name="branch_merge",
    )(*branches, wb, z, z, z)


SCALE = HEAD_DIM ** -0.5
NSA_REP = NSA_H // NSA_KVH
CMP_PER_PAGE = PAGE_SIZE // CMP_STRIDE
CMP_HALF = CMP_LEN // CMP_STRIDE * 0 + CMP_STRIDE * HEAD_DIM


def _bf(x):
    return x.astype(jnp.bfloat16)


def _split3(x):
    hi = _bf(x)
    r1 = x - hi.astype(jnp.float32)
    mid = _bf(r1)
    lo = _bf(r1 - mid.astype(jnp.float32))
    return hi, mid, lo


_NT = (((1,), (1,)), ((), ()))


def _dot(a, b):
    return jnp.dot(a, b, preferred_element_type=jnp.float32)


def _dot_nt(a, b):
    return lax.dot_general(a, b, _NT, preferred_element_type=jnp.float32)


def _dot_f32_lhs(a, b_exact):
    return sum(_dot(p, b_exact) for p in _split3(a))


def _dot_f32_rhs(a_exact, b):
    return sum(_dot(a_exact, p) for p in _split3(b))


def _dot_nt_f32(a, b):
    a1, a2, a3 = _split3(a)
    b1, b2, b3 = _split3(b)
    return (_dot_nt(a1, b1) + (_dot_nt(a1, b2) + _dot_nt(a2, b1))
            + (_dot_nt(a1, b3) + _dot_nt(a2, b2) + _dot_nt(a3, b1)))


def _masked_softmax(s, mask):
    s = jnp.where(mask, s, NEG)
    m = jnp.max(s, axis=-1, keepdims=True)
    e = jnp.where(mask, jnp.exp(s - m), 0.0)
    l = jnp.sum(e, axis=-1, keepdims=True)
    return e / jnp.where(l > 0.0, l, 1.0)


def _masked_softmax_rows(s, mask):
    s = jnp.where(mask, s, NEG)
    m = jnp.max(s, axis=0, keepdims=True)
    e = jnp.where(mask, jnp.exp(s - m), 0.0)
    l = jnp.sum(e, axis=0, keepdims=True)
    return e / jnp.where(l > 0.0, l, 1.0)


def _rank_lanes(score, n):
    lane = lax.broadcasted_iota(jnp.int32, score.shape, 1)
    rank = jnp.zeros(score.shape, jnp.float32)
    for i in range(n):
        col = score[:, i:i + 1]
        ahead = jnp.where(col == score, jnp.where(lane > i, 1.0, 0.0), jnp.where(col > score, 1.0, 0.0))
        rank = rank + ahead
    return rank


def _rank_sublanes(score, n):
    row = lax.broadcasted_iota(jnp.int32, score.shape, 0)
    rank = jnp.zeros(score.shape, jnp.float32)
    for i in range(n):
        r = score[i:i + 1, :]
        ahead = jnp.where(r == score, jnp.where(row > i, 1.0, 0.0), jnp.where(r > score, 1.0, 0.0))
        rank = rank + ahead
    return rank


def _compress_tokens(x_bf, w1ab_ref, w2_ref, posrows_ref):
    n = x_bf.shape[0]
    w1ab = w1ab_ref[...]
    pre = _dot(x_bf, w1ab)
    pb = _dot(posrows_ref[...], w1ab)
    posbias = pb[0:1, :HEAD_DIM] + pb[1:2, HEAD_DIM:]
    nxt = pltpu.roll(pre[:, HEAD_DIM:], n - 1, 0)
    hid = pre[:, :HEAD_DIM] + nxt + posbias
    return _dot(_bf(jax.nn.silu(hid)), w2_ref[...])


def prep_compress_weights(cmp_pos, cmp_w1, cmp_w2):
    half = CMP_STRIDE * HEAD_DIM
    w1ab = jnp.concatenate([cmp_w1[:, :half], cmp_w1[:, half:]], axis=-1)
    pos2 = cmp_pos.reshape(2, 2, half)
    posrows = jnp.concatenate([pos2, jnp.zeros((2, 6, half), cmp_pos.dtype)], axis=1)
    return _bf(w1ab), _bf(cmp_w2), _bf(posrows)


def _cmp_prompt_kernel(x_ref, w1ab_ref, w2_ref, posrows_ref, o_ref):
    n_chunks = x_ref.shape[0] // CMP_STRIDE
    xc = jnp.concatenate(
        [_bf(x_ref[pl.ds(j, n_chunks, stride=CMP_STRIDE), :]) for j in range(CMP_STRIDE)], axis=-1)
    o_ref[...] = _compress_tokens(xc, w1ab_ref, w2_ref, posrows_ref)


def nsa_compress_prompt(z, b, t, cw):
    w1ab, w2, posrows = cw
    n_chunks = t // CMP_STRIDE
    kind_spec = lambda a: pl.BlockSpec((None,) + a.shape[1:], lambda i, c: (c // NSA_KVH, 0, 0))
    return pl.pallas_call(
        _cmp_prompt_kernel,
        grid=(b, 2 * NSA_KVH),
        in_specs=[pl.BlockSpec((t, HEAD_DIM), lambda i, c: (i, OFF_NKV // HEAD_DIM + c)),
                  kind_spec(w1ab), kind_spec(w2), kind_spec(posrows)],
        out_specs=pl.BlockSpec((None, None, n_chunks, HEAD_DIM), lambda i, c: (i, c, 0, 0)),
        out_shape=jax.ShapeDtypeStruct((b, 2 * NSA_KVH, n_chunks, HEAD_DIM), jnp.float32),
        compiler_params=_cparams(("parallel", "parallel")),
        name="nsa_compress_prompt",
    )(z, w1ab, w2, posrows)


NSA_TQ = 128


def _nsa_prompt_kernel(q_ref, ckv_ref, ks_ref, vs_ref, kw_ref, vw_ref, gate_ref, ovl_ref, exp_ref, o_ref,
                       ks_bf, vs_bf, kw_bf, vw_bf, *, t):
    qt = pl.program_id(1)
    q0 = pl.multiple_of(qt * NSA_TQ, NSA_TQ)
    n_cmp = ckv_ref.shape[1]
    n_sel = t // SEL_BLOCK
    span = WINDOW + NSA_TQ

    @pl.when(qt == 0)
    def _():
        ks_bf[...] = _bf(ks_ref[...])
        vs_bf[...] = _bf(vs_ref[...])
        kw_bf[...] = _bf(kw_ref[...])
        vw_bf[...] = _bf(vw_ref[...])

    qpos = q0 + lax.broadcasted_iota(jnp.int32, (NSA_TQ, 1), 0)
    gates = jax.nn.sigmoid(gate_ref[...])
    lane = lax.broadcasted_iota(jnp.int32, (NSA_TQ, 128), 1)
    lane_c = lax.broadcasted_iota(jnp.int32, (NSA_TQ, n_cmp), 1)
    cmask = (lane_c * CMP_STRIDE + (CMP_LEN - 1) <= qpos) & (lane_c < n_cmp - 1)
    cur = qpos // SEL_BLOCK
    forced = (lane == 0) | (lane == cur) | (lane == cur - 1)
    kpos = lax.broadcasted_iota(jnp.int32, (NSA_TQ, t), 1)
    wstart = pl.multiple_of(jnp.maximum(q0 - WINDOW, 0), NSA_TQ)
    wd = qpos - (wstart + lax.broadcasted_iota(jnp.int32, (NSA_TQ, span), 1))
    wmask = (wd >= 0) & (wd <= WINDOW)

    def attend(qs, k_bf, v_bf, mask):
        nk = k_bf.shape[0]
        s = _dot_nt(qs, k_bf).reshape(NSA_REP, NSA_TQ, nk) * SCALE
        p = _masked_softmax(s, mask[None])
        return p, _dot(_bf(p.reshape(NSA_REP * NSA_TQ, nk)), v_bf)

    for g in range(NSA_KVH):
        qs = _bf(jnp.concatenate(
            [q_ref[:, (g * NSA_REP + r) * HEAD_DIM:(g * NSA_REP + r + 1) * HEAD_DIM] for r in range(NSA_REP)],
            axis=0))
        p_cmp, o_cmp = attend(qs, _bf(ckv_ref[g]), _bf(ckv_ref[NSA_KVH + g]), cmask)
        psum = p_cmp[0] + p_cmp[1] + p_cmp[2] + p_cmp[3]
        imp = _dot_f32_lhs(psum, ovl_ref[...])
        score = jnp.where(lane <= cur, imp + jnp.where(forced, FORCE_BONUS, 0.0), NEG)
        rank = _rank_lanes(score, n_sel)
        n_ok = jnp.minimum(cur + 1, min(SEL_TOPK, n_sel)).astype(jnp.float32)
        sel = jnp.where(rank < n_ok, 1.0, 0.0)
        keysel = _dot(_bf(sel), exp_ref[...])
        smask = jnp.where(kpos <= qpos, keysel, 0.0) > 0.5
        gsl = slice(g * HEAD_DIM, (g + 1) * HEAD_DIM)
        _, o_slc = attend(qs, ks_bf[:, gsl], vs_bf[:, gsl], smask)
        _, o_win = attend(qs, kw_bf[pl.ds(wstart, span), gsl], vw_bf[pl.ds(wstart, span), gsl], wmask)
        for r in range(NSA_REP):
            h = g * NSA_REP + r
            rows = slice(r * NSA_TQ, (r + 1) * NSA_TQ)
            o = (gates[:, h:h + 1] * o_cmp[rows] + gates[:, NSA_H + h:NSA_H + h + 1] * o_slc[rows]
                 + gates[:, 2 * NSA_H + h:2 * NSA_H + h + 1] * o_win[rows])
            o_ref[:, h * HEAD_DIM:(h + 1) * HEAD_DIM] = o.astype(o_ref.dtype)


def nsa_prompt(z, ckv, b, t):
    n_cmp = ckv.shape[2]
    n_sel = t // SEL_BLOCK
    nqt = t // NSA_TQ
    gw = NSA_KVH * HEAD_DIM
    cmp_start = np.arange(n_cmp) * CMP_STRIDE
    sel_start = np.arange(128) * SEL_BLOCK
    ovl = ((cmp_start[:, None] < sel_start[None, :] + SEL_BLOCK) & (cmp_start[:, None] + CMP_LEN > sel_start[None, :])
           & (np.arange(n_cmp)[:, None] < n_cmp - 1) & (np.arange(128)[None, :] < n_sel))
    ovl = jnp.asarray(ovl, jnp.bfloat16)
    expand = jnp.asarray(np.arange(128)[:, None] == (np.arange(t)[None, :] // SEL_BLOCK), jnp.bfloat16)
    kv_spec = lambda kind: pl.BlockSpec((t, gw), lambda i, j: (i, OFF_NKV // gw + kind))
    return pl.pallas_call(
        functools.partial(_nsa_prompt_kernel, t=t),
        grid=(b, nqt),
        in_specs=[pl.BlockSpec((NSA_TQ, N_NQ), lambda i, j: (i * nqt + j, OFF_NQ // N_NQ)),
                  pl.BlockSpec((None, 2 * NSA_KVH, n_cmp, HEAD_DIM), lambda i, j: (i, 0, 0, 0)),
                  kv_spec(2), kv_spec(3), kv_spec(4), kv_spec(5),
                  pl.BlockSpec((NSA_TQ, 128), lambda i, j: (i * nqt + j, OFF_NGATE // 128)),
                  pl.BlockSpec(ovl.shape, lambda i, j: (0, 0)),
                  pl.BlockSpec(expand.shape, lambda i, j: (0, 0))],
        out_specs=pl.BlockSpec((NSA_TQ, MIX_W), lambda i, j: (i * nqt + j, 0)),
        out_shape=jax.ShapeDtypeStruct((b * t, MIX_W), jnp.bfloat16),
        scratch_shapes=[pltpu.VMEM((t, gw), jnp.bfloat16)] * 4,
        compiler_params=_cparams(("parallel", "arbitrary")),
        name="nsa_prompt",
    )(z, ckv, z, z, z, z, z, ovl, expand)


def _moba_prompt_kernel(q_ref, k_ref, v_ref, exp_ref, o_ref, k_bf, v_bf, km_ref, *, t):
    qt = pl.program_id(2)
    nb = t // MOBA_BLOCK

    @pl.when(qt == 0)
    def _():
        k = k_ref[...]
        k_bf[...] = _bf(k)
        v_bf[...] = _bf(v_ref[...])
        km_ref[...] = jnp.zeros(km_ref.shape, jnp.float32)
        km_ref[0:nb, :] = jnp.mean(k.reshape(nb, MOBA_BLOCK, HEAD_DIM), axis=1)

    q = q_ref[...]
    qpos = qt * MOBA_BLOCK + lax.broadcasted_iota(jnp.int32, (MOBA_BLOCK, 1), 0)
    lane = lax.broadcasted_iota(jnp.int32, (MOBA_BLOCK, 128), 1)
    gate = jnp.where(lane < qt, _dot_nt_f32(q, km_ref[...]), NEG)
    rank = _rank_lanes(gate, nb)
    n_ok = jnp.minimum(qt, min(MOBA_TOPK, nb)).astype(jnp.float32)
    sel = jnp.where(rank < n_ok, 1.0, 0.0)
    keysel = _dot(_bf(sel), exp_ref[...])
    kpos = lax.broadcasted_iota(jnp.int32, (MOBA_BLOCK, t), 1)
    own = (kpos // MOBA_BLOCK == qt) & (kpos <= qpos)
    mask = jnp.where(own, 1.0, keysel) > 0.5
    s = _dot_nt(_bf(q), k_bf[...]) * SCALE
    p = _masked_softmax(s, mask)
    o_ref[...] = _dot(_bf(p), v_bf[...]).astype(o_ref.dtype)


def moba_prompt(z, b, t):
    nqt = t // MOBA_BLOCK
    expand = jnp.asarray(np.arange(128)[:, None] == (np.arange(t)[None, :] // MOBA_BLOCK), jnp.bfloat16)
    col = lambda part: OFF_MQKV // HEAD_DIM + part * MOBA_H
    return pl.pallas_call(
        functools.partial(_moba_prompt_kernel, t=t),
        grid=(b, MOBA_H, nqt),
        in_specs=[pl.BlockSpec((MOBA_BLOCK, HEAD_DIM), lambda i, h, j: (i * nqt + j, col(0) + h)),
                  pl.BlockSpec((t, HEAD_DIM), lambda i, h, j: (i, col(1) + h)),
                  pl.BlockSpec((t, HEAD_DIM), lambda i, h, j: (i, col(2) + h)),
                  pl.BlockSpec(expand.shape, lambda i, h, j: (0, 0))],
        out_specs=pl.BlockSpec((MOBA_BLOCK, HEAD_DIM), lambda i, h, j: (i * nqt + j, h)),
        out_shape=jax.ShapeDtypeStruct((b * t, MIX_W), jnp.bfloat16),
        scratch_shapes=[pltpu.VMEM((t, HEAD_DIM), jnp.bfloat16), pltpu.VMEM((t, HEAD_DIM), jnp.bfloat16),
                        pltpu.VMEM((128, HEAD_DIM), jnp.float32)],
        compiler_params=_cparams(("parallel", "parallel", "arbitrary")),
        name="moba_prompt",
    )(z, z, z, expand)


HG_KCHUNK = 128
HG_SUB = 16


def _hgrn_gates(hq, hf, lb):
    q = jax.nn.silu(hq)
    f = lb + (1.0 - lb) * jax.nn.sigmoid(hf)
    logf = jnp.log(jnp.maximum(f, F_FLOOR))
    k = (1.0 - lb) * jax.nn.sigmoid(-hf)
    return q, k, logf


def _hgrn_out(o, hg, norm_w):
    o = o * lax.rsqrt(jnp.mean(o * o, axis=-1, keepdims=True) + EPS)
    return o * norm_w * jax.nn.silu(hg)


def _hgrn_prompt_kernel(hq_ref, hf_ref, hi_ref, hg_ref, lb_ref, nw_ref, tri_ref, o_ref, s_ref,
                        st_ref, k_sc, cum_sc, v_sc, *, t):
    c = HG_KCHUNK
    lb = lb_ref[...]
    nw = nw_ref[...]
    st_ref[...] = jnp.zeros(st_ref.shape, jnp.float32)
    row_s = lax.broadcasted_iota(jnp.int32, (HG_SUB, 1), 0)
    col_c = lax.broadcasted_iota(jnp.int32, (HG_SUB, c), 1)

    def chunk(ci, carry):
        rows = pl.ds(pl.multiple_of(ci * c, c), c)
        q, k, logf = _hgrn_gates(hq_ref[rows, :], hf_ref[rows, :], lb)
        v = hi_ref[rows, :]
        cum = _dot_f32_rhs(tri_ref[...], logf)
        k_sc[...] = k
        cum_sc[...] = cum
        v_sc[...] = v
        v_b = _bf(v)
        st = st_ref[...]
        o_carry = _dot_nt(_bf(q * jnp.exp(cum)), _bf(st))
        o_parts = []
        for i in range(c // HG_SUB):
            r0 = i * HG_SUB
            q_i = q[r0:r0 + HG_SUB]
            cum_i = cum[r0:r0 + HG_SUB]
            o_i = o_carry[r0:r0 + HG_SUB]
            if i > 0:
                edge = cum[r0 - 1:r0]
                k_e = k * jnp.exp(jnp.minimum(edge - cum, 0.0))
                a = _dot_nt(_bf(q_i * jnp.exp(cum_i - edge)), _bf(k_e))
                o_i = o_i + _dot(_bf(jnp.where(col_c < r0, a, 0.0)), v_b)
            for s in range(r0, r0 + HG_SUB):
                d = jnp.minimum(cum_i - cum_sc[pl.ds(s, 1), :], 0.0)
                w = jnp.sum(q_i * jnp.exp(d) * k_sc[pl.ds(s, 1), :], axis=-1, keepdims=True)
                o_i = o_i + jnp.where(row_s >= s - r0, w, 0.0) * v_sc[pl.ds(s, 1), :]
            o_parts.append(o_i)
        o = jnp.concatenate(o_parts, axis=0)
        o_ref[rows, :] = _hgrn_out(o, hg_ref[rows, :], nw).astype(o_ref.dtype)
        last = cum[c - 1:c]
        kd = k * jnp.exp(last - cum)
        upd = lax.dot_general(v_b, _bf(kd), (((0,), (0,)), ((), ())), preferred_element_type=jnp.float32)
        st_ref[...] = st * jnp.exp(last) + upd
        return carry

    lax.fori_loop(0, t // c, chunk, 0)
    s_ref[...] = st_ref[...].T


def hgrn_prompt(z, lb, norm_w, b, t):
    tri = jnp.asarray(np.tril(np.ones((HG_KCHUNK, HG_KCHUNK))), jnp.bfloat16)
    col = lambda off: (lambda i, h: (i, off // HEAD_DIM + h))
    vec = pl.BlockSpec((1, HEAD_DIM), lambda i, h: (0, h))
    return pl.pallas_call(
        functools.partial(_hgrn_prompt_kernel, t=t),
        grid=(b, HG_H),
        in_specs=[pl.BlockSpec((t, HEAD_DIM), col(OFF_HQ)), pl.BlockSpec((t, HEAD_DIM), col(OFF_HF)),
                  pl.BlockSpec((t, HEAD_DIM), col(OFF_HI)), pl.BlockSpec((t, HEAD_DIM), col(OFF_HGATE)),
                  vec, vec, pl.BlockSpec(tri.shape, lambda i, h: (0, 0))],
        out_specs=[pl.BlockSpec((t, HEAD_DIM), lambda i, h: (i, h)),
                   pl.BlockSpec((None, None, HG_DK, HG_DV), lambda i, h: (i, h, 0, 0))],
        out_shape=[jax.ShapeDtypeStruct((b * t, MIX_W), jnp.bfloat16),
                   jax.ShapeDtypeStruct((b, HG_H, HG_DK, HG_DV), jnp.float32)],
        scratch_shapes=[pltpu.VMEM((HG_DV, HG_DK), jnp.float32)] + [pltpu.VMEM((HG_KCHUNK, HG_DK), jnp.float32)] * 3,
        compiler_params=_cparams(("parallel", "parallel")),
        name="hgrn_prompt",
    )(z, z, z, z, lb.reshape(1, MIX_W), norm_w.reshape(1, MIX_W), tri)


SEL_ROWS = 40


def _qbd(q, heads_per_kv):
    n_kv = q.shape[0] // heads_per_kv
    row = lax.broadcasted_iota(jnp.int32, q.shape, 0)
    blocks = [jnp.where(row // heads_per_kv == g, q, 0.0) for g in range(n_kv)]
    top = jnp.concatenate(blocks, axis=1)
    return jnp.concatenate([top, jnp.zeros((128 - q.shape[0], top.shape[1]), jnp.float32)], axis=0)


def _softmax_keys(s_sc, s_new, new_ok):
    s = s_sc[...]
    s_new = jnp.where(new_ok, s_new, NEG)
    m = jnp.maximum(jnp.max(s, axis=0, keepdims=True), s_new)
    e = jnp.where(s > 0.5 * NEG, jnp.exp(s - m), 0.0)
    e_new = jnp.where(new_ok, jnp.exp(s_new - m), 0.0)
    l = jnp.sum(e, axis=0, keepdims=True) + e_new
    inv = 1.0 / jnp.where(l > 0.0, l, 1.0)
    return e * inv, e_new * inv


def _weighted_values(p_page, v_page, acc, heads_per_kv):
    rows = p_page.shape[0]
    out = []
    for h in range(len(acc)):
        g = h // heads_per_kv
        prod = p_page[:, h:h + 1] * v_page[:, g * HEAD_DIM:(g + 1) * HEAD_DIM]
        out.append(acc[h] + jnp.sum(prod.reshape(rows // 8, 8, HEAD_DIM), axis=0))
    return out


def _finish_values(acc, p_new, v_new, heads_per_kv):
    rows = []
    for h in range(len(acc)):
        g = h // heads_per_kv
        rows.append(jnp.sum(acc[h], axis=0, keepdims=True) + p_new[:, h:h + 1] * v_new[:, g * HEAD_DIM:(g + 1) * HEAD_DIM])
    return jnp.concatenate(rows, axis=0)


def _cmp_decode_kernel(pt_ref, *refs, n_pages, pos):
    pages = refs[:n_pages]
    q_ref, w1ab_ref, w2_ref, posrows_ref, ovlt_ref, grp_ref, o_ref, sel_ref, xs = refs[n_pages:]
    n_tok = n_pages * CMP_PER_PAGE
    for pp in range(n_pages // 2):
        for j in range(CMP_STRIDE):
            for c in range(2 * NSA_KVH):
                lanes = slice((j * 2 * NSA_KVH + c) * HEAD_DIM, (j * 2 * NSA_KVH + c + 1) * HEAD_DIM)
                piece = jnp.concatenate([pages[2 * pp][:, lanes], pages[2 * pp + 1][:, lanes]], axis=0)
                xs[c, 2 * CMP_PER_PAGE * pp:2 * CMP_PER_PAGE * (pp + 1), j * HEAD_DIM:(j + 1) * HEAD_DIM] = _bf(piece)
    kc = [_compress_tokens(xs[g], w1ab_ref.at[0], w2_ref.at[0], posrows_ref.at[0]) for g in range(NSA_KVH)]
    vc = [_compress_tokens(xs[NSA_KVH + g], w1ab_ref.at[1], w2_ref.at[1], posrows_ref.at[1])
          for g in range(NSA_KVH)]
    qbd = _bf(_qbd(q_ref[...], NSA_REP))
    s = _dot_nt(_bf(jnp.concatenate(kc, axis=1)), qbd) * SCALE
    tok = lax.broadcasted_iota(jnp.int32, s.shape, 0)
    p = _masked_softmax_rows(s, (tok * CMP_STRIDE + (CMP_LEN - 1) <= pos) & (tok < n_tok - 1))
    o_ref[...] = jnp.concatenate(
        [jnp.sum(p[:, h:h + 1] * vc[h // NSA_REP], axis=0, keepdims=True) for h in range(NSA_H)], axis=0)
    imp = _dot_f32_rhs(ovlt_ref[...], _dot_f32_lhs(p, grp_ref[...]))
    blk = lax.broadcasted_iota(jnp.int32, imp.shape, 0)
    cur = pos // SEL_BLOCK
    forced = (blk == 0) | (blk == cur) | (blk == cur - 1)
    score = jnp.where(blk <= cur, imp + jnp.where(forced, FORCE_BONUS, 0.0), NEG)
    n_sel = cur + 1
    rank = _rank_sublanes(score, n_sel)
    sel_ref[...] = jnp.where(rank < float(min(SEL_TOPK, n_sel)), 1.0, 0.0)


def _page_specs(n_pages, layer, block, col=None):
    def spec(p):
        if col is None:
            return pl.BlockSpec((None, None) + block, lambda b, pt: (layer, pt[b * n_pages + p], 0, 0))
        return pl.BlockSpec((None, None) + block, lambda b, ph, pt: (layer, pt[b * n_pages + p], 0, ph))
    return [spec(p) for p in range(n_pages)]


def nsa_cmp_decode(cache_cmp, layer, pt_flat, q_s, cw, n_pages):
    w1ab, w2, posrows = cw
    b = q_s.shape[0]
    pos = n_pages * PAGE_SIZE
    n_tok = n_pages * CMP_PER_PAGE
    lanes = CMP_STRIDE * 2 * NSA_KVH * HEAD_DIM
    view = cache_cmp.reshape(cache_cmp.shape[0], cache_cmp.shape[1], CMP_PER_PAGE, lanes)
    cmp_start = np.arange(n_tok) * CMP_STRIDE
    sel_start = np.arange(SEL_ROWS) * SEL_BLOCK
    ovlt = ((cmp_start[None, :] < sel_start[:, None] + SEL_BLOCK) & (cmp_start[None, :] + CMP_LEN > sel_start[:, None])
            & (np.arange(n_tok)[None, :] < n_tok - 1))
    hh = np.arange(128)
    grp = (hh[:, None] // NSA_REP == hh[None, :] // NSA_REP) & (hh[:, None] < NSA_H) & (hh[None, :] < NSA_H)
    ovlt, grp = jnp.asarray(ovlt, jnp.bfloat16), jnp.asarray(grp, jnp.bfloat16)
    const = lambda a: pl.BlockSpec(a.shape, lambda i, pt: (0,) * a.ndim)
    grid_spec = pltpu.PrefetchScalarGridSpec(
        num_scalar_prefetch=1, grid=(b,),
        in_specs=_page_specs(n_pages, layer, (CMP_PER_PAGE, lanes))
        + [pl.BlockSpec((None, NSA_H, HEAD_DIM), lambda i, pt: (i, 0, 0)),
           const(w1ab), const(w2), const(posrows), const(ovlt), const(grp)],
        out_specs=[pl.BlockSpec((None, NSA_H, HEAD_DIM), lambda i, pt: (i, 0, 0)),
                   pl.BlockSpec((None, SEL_ROWS, 128), lambda i, pt: (i, 0, 0))],
        scratch_shapes=[pltpu.VMEM((2 * NSA_KVH, n_tok, CMP_STRIDE * HEAD_DIM), jnp.bfloat16)])
    return pl.pallas_call(
        functools.partial(_cmp_decode_kernel, n_pages=n_pages, pos=pos),
        grid_spec=grid_spec,
        out_shape=[jax.ShapeDtypeStruct((b, NSA_H, HEAD_DIM), jnp.float32),
                   jax.ShapeDtypeStruct((b, SEL_ROWS, 128), jnp.float32)],
        compiler_params=_cparams(("parallel",)),
        name="nsa_cmp_decode",
    )(pt_flat, *([view] * n_pages), q_s, w1ab, w2, posrows, ovlt, grp)


def _slc_decode_kernel(pt_ref, *refs, n_pages, pos):
    pages = refs[:n_pages]
    q_ref, kv_ref, sel_ref, o_ref, s_sc = refs[n_pages:]
    kw = NSA_KVH * HEAD_DIM
    qbd = _bf(_qbd(q_ref[...], NSA_REP))
    row = lax.broadcasted_iota(jnp.int32, (PAGE_SIZE, 128), 0)
    per_page = PAGE_SIZE // SEL_BLOCK
    for p in range(n_pages):
        s = _dot_nt(_bf(pages[p][:, 0:kw]), qbd) * SCALE
        sel = sel_ref[pl.ds(per_page * p, 1), :]
        for i in range(1, per_page):
            sel = jnp.where(row >= i * SEL_BLOCK, sel_ref[pl.ds(per_page * p + i, 1), :], sel)
        kpos = p * PAGE_SIZE + row
        s_sc[p * PAGE_SIZE:(p + 1) * PAGE_SIZE, :] = jnp.where(kpos <= pos, jnp.where(sel > 0.5, s, NEG), NEG)
    k_new = kv_ref[:, 2 * kw:3 * kw]
    v_new = kv_ref[:, 3 * kw:4 * kw]
    s_new = _dot_nt(_bf(jnp.broadcast_to(k_new, (8, kw))), qbd)[0:1] * SCALE
    new_ok = sel_ref[pl.ds(pos // SEL_BLOCK, 1), :] > 0.5
    _softmax_keys_into(s_sc, s_new, new_ok, o_ref, pages, kw, v_new, NSA_REP, n_pages, PAGE_SIZE)


def _softmax_keys_into(s_sc, s_new, new_ok, o_ref, pages, v_lane0, v_new, heads_per_kv, n_pages, rows):
    p_all, p_new = _softmax_keys(s_sc, s_new, new_ok)
    s_sc[...] = p_all
    n_kv = o_ref.shape[0] // heads_per_kv
    acc = [jnp.zeros((8, HEAD_DIM), jnp.float32)] * o_ref.shape[0]
    for p in range(n_pages):
        acc = _weighted_values(s_sc[p * rows:(p + 1) * rows, :], pages[p][:, v_lane0:v_lane0 + n_kv * HEAD_DIM],
                               acc, heads_per_kv)
    o_ref[...] = _finish_values(acc, p_new, v_new, heads_per_kv).astype(o_ref.dtype)


def nsa_slc_decode(cache_slc, layer, pt_flat, q_s, kv_new, sel, n_pages):
    b = q_s.shape[0]
    pos = n_pages * PAGE_SIZE
    lanes = 2 * NSA_KVH * HEAD_DIM
    view = cache_slc.reshape(cache_slc.shape[0], cache_slc.shape[1], PAGE_SIZE, lanes)
    grid_spec = pltpu.PrefetchScalarGridSpec(
        num_scalar_prefetch=1, grid=(b,),
        in_specs=_page_specs(n_pages, layer, (PAGE_SIZE, lanes))
        + [pl.BlockSpec((None, NSA_H, HEAD_DIM), lambda i, pt: (i, 0, 0)),
           pl.BlockSpec((None, 1, N_NKV), lambda i, pt: (i, 0, 0)),
           pl.BlockSpec((None, SEL_ROWS, 128), lambda i, pt: (i, 0, 0))],
        out_specs=pl.BlockSpec((None, NSA_H, HEAD_DIM), lambda i, pt: (i, 0, 0)),
        scratch_shapes=[pltpu.VMEM((n_pages * PAGE_SIZE, 128), jnp.float32)])
    return pl.pallas_call(
        functools.partial(_slc_decode_kernel, n_pages=n_pages, pos=pos),
        grid_spec=grid_spec,
        out_shape=jax.ShapeDtypeStruct((b, NSA_H, HEAD_DIM), jnp.float32),
        compiler_params=_cparams(("parallel",)),
        name="nsa_slc_decode",
    )(pt_flat, *([view] * n_pages), q_s, kv_new, sel)


def _win_decode_kernel(win_ref, q_ref, kv_ref, ocmp_ref, oslc_ref, gate_ref, o_ref, s_sc, ow_sc, *, pos):
    kw = NSA_KVH * HEAD_DIM
    wb = win_ref.shape[0]
    qbd = _bf(_qbd(q_ref[...], NSA_REP))
    s = _dot_nt(_bf(win_ref[:, 0:kw]), qbd) * SCALE
    dist = pos - (pos - wb + lax.broadcasted_iota(jnp.int32, s.shape, 0))
    s_sc[...] = jnp.where(dist >= 0, jnp.where(dist <= WINDOW, s, NEG), NEG)
    k_new = kv_ref[:, 4 * kw:5 * kw]
    v_new = kv_ref[:, 5 * kw:6 * kw]
    s_new = _dot_nt(_bf(jnp.broadcast_to(k_new, (8, kw))), qbd)[0:1] * SCALE
    new_ok = jnp.ones(s_new.shape, jnp.float32) > 0.5
    _softmax_keys_into(s_sc, s_new, new_ok, ow_sc, [win_ref], kw, v_new, NSA_REP, 1, wb)
    g = jax.nn.sigmoid(gate_ref[...])
    o_ref[...] = (g[0] * ocmp_ref[...] + g[1] * oslc_ref[...] + g[2] * ow_sc[...]).astype(o_ref.dtype)


def nsa_win_decode(cache_win, layer, q_s, kv_new, o_cmp, o_slc, gate_cols, pos):
    b, wb = cache_win.shape[1], cache_win.shape[2]
    lanes = 2 * NSA_KVH * HEAD_DIM
    view = cache_win.reshape(cache_win.shape[0], b, wb, lanes)
    hspec = pl.BlockSpec((None, NSA_H, HEAD_DIM), lambda i: (i, 0, 0))
    return pl.pallas_call(
        functools.partial(_win_decode_kernel, pos=pos),
        grid=(b,),
        in_specs=[pl.BlockSpec((None, None, wb, lanes), lambda i: (layer, i, 0, 0)), hspec,
                  pl.BlockSpec((None, 1, N_NKV), lambda i: (i, 0, 0)), hspec, hspec,
                  pl.BlockSpec((None, 3, NSA_H, 1), lambda i: (i, 0, 0, 0))],
        out_specs=hspec,
        out_shape=jax.ShapeDtypeStruct((b, NSA_H, HEAD_DIM), jnp.bfloat16),
        scratch_shapes=[pltpu.VMEM((wb, 128), jnp.float32), pltpu.VMEM((NSA_H, HEAD_DIM), jnp.float32)],
        compiler_params=_cparams(("parallel",)),
        name="nsa_win_decode",
    )(view, q_s, kv_new, o_cmp, o_slc, gate_cols)


def _moba_decode_kernel(pt_ref, *refs, n_pages, pos):
    pages = refs[:n_pages]
    q_ref, kn_ref, vn_ref, o_ref, s_sc, pn_sc = refs[n_pages:]
    phase = pl.program_id(1)
    per_blk = MOBA_BLOCK // PAGE_SIZE
    nb = n_pages // per_blk

    @pl.when(phase == 0)
    def _():
        qf = _qbd(q_ref[...], 1)
        qbd = _bf(qf)
        ksum = []
        for p in range(n_pages):
            k = pages[p][...]
            s_sc[p * PAGE_SIZE:(p + 1) * PAGE_SIZE, :] = _dot_nt(_bf(k), qbd) * SCALE
            ksum.append(jnp.sum(k, axis=0, keepdims=True))
        kmean = jnp.concatenate(
            [sum(ksum[n * per_blk:(n + 1) * per_blk]) * (1.0 / MOBA_BLOCK) for n in range(nb)], axis=0)
        blk = lax.broadcasted_iota(jnp.int32, (nb, 128), 0)
        cur = pos // MOBA_BLOCK
        gate = jnp.where(blk < cur, _dot_nt_f32(kmean, qf), NEG)
        rank = _rank_sublanes(gate, nb)
        sel = jnp.where(rank < float(min(cur, MOBA_TOPK, nb)), 1.0, 0.0)
        row = lax.broadcasted_iota(jnp.int32, (PAGE_SIZE, 128), 0)
        for p in range(n_pages):
            ok = (jnp.broadcast_to(sel[p // per_blk:p // per_blk + 1, :], (PAGE_SIZE, 128)) > 0.5)
            kpos = p * PAGE_SIZE + row
            rows = slice(p * PAGE_SIZE, (p + 1) * PAGE_SIZE)
            s_sc[rows, :] = jnp.where(kpos <= pos, jnp.where(ok, s_sc[rows, :], NEG), NEG)
        s_new = _dot_nt(_bf(jnp.broadcast_to(kn_ref[...], (8, MIX_W))), qbd)[0:1] * SCALE
        new_ok = jnp.ones(s_new.shape, jnp.float32) > 0.5
        p_all, p_new = _softmax_keys(s_sc, s_new, new_ok)
        s_sc[...] = p_all
        pn_sc[...] = jnp.broadcast_to(p_new, pn_sc.shape)

    @pl.when(phase == 1)
    def _():
        acc = [jnp.zeros((8, HEAD_DIM), jnp.float32)] * MOBA_H
        for p in range(n_pages):
            acc = _weighted_values(s_sc[p * PAGE_SIZE:(p + 1) * PAGE_SIZE, :], pages[p][...], acc, 1)
        o_ref[...] = _finish_values(acc, pn_sc[0:1, :], vn_ref[...], 1).astype(o_ref.dtype)


def moba_decode(cache_moba, layer, pt_flat, q_m, k_new, v_new, n_pages):
    b = q_m.shape[0]
    pos = n_pages * PAGE_SIZE
    view = cache_moba.reshape(cache_moba.shape[0], cache_moba.shape[1], PAGE_SIZE, 2 * MIX_W)
    grid_spec = pltpu.PrefetchScalarGridSpec(
        num_scalar_prefetch=1, grid=(b, 2),
        in_specs=_page_specs(n_pages, layer, (PAGE_SIZE, MIX_W), col="phase")
        + [pl.BlockSpec((None, MOBA_H, HEAD_DIM), lambda i, ph, pt: (i, 0, 0)),
           pl.BlockSpec((None, 1, MIX_W), lambda i, ph, pt: (i, 0, 0)),
           pl.BlockSpec((None, 1, MIX_W), lambda i, ph, pt: (i, 0, 0))],
        out_specs=pl.BlockSpec((None, MOBA_H, HEAD_DIM), lambda i, ph, pt: (i, 0, 0)),
        scratch_shapes=[pltpu.VMEM((n_pages * PAGE_SIZE, 128), jnp.float32), pltpu.VMEM((8, 128), jnp.float32)])
    return pl.pallas_call(
        functools.partial(_moba_decode_kernel, n_pages=n_pages, pos=pos),
        grid_spec=grid_spec,
        out_shape=jax.ShapeDtypeStruct((b, MOBA_H, HEAD_DIM), jnp.bfloat16),
        compiler_params=_cparams(("parallel", "arbitrary")),
        name="moba_decode",
    )(pt_flat, *([view] * n_pages), q_m, k_new, v_new)


def _hgrn_decode_kernel(s_ref, hq_ref, hf_ref, hi_ref, hg_ref, lb_ref, nw_ref, o_ref, so_ref):
    for h in range(HG_H):
        lanes = slice(h * HG_DV, (h + 1) * HG_DV)
        q, k, logf = _hgrn_gates(hq_ref[h], hf_ref[h], lb_ref[h])
        s_new = jnp.exp(logf) * s_ref[h] + k * hi_ref[:, lanes]
        so_ref[h] = s_new
        o = jnp.sum(q * s_new, axis=0, keepdims=True)
        o_ref[:, lanes] = _hgrn_out(o, hg_ref[:, lanes], nw_ref[:, lanes]).astype(o_ref.dtype)


def hgrn_decode(state, layer, hq_col, hf_col, hi_row, hg_row, lb, norm_w):
    b = state.shape[1]
    colspec = pl.BlockSpec((None, HG_H, HG_DK, 1), lambda i: (i, 0, 0, 0))
    rowspec = pl.BlockSpec((None, 1, MIX_W), lambda i: (i, 0, 0))
    return pl.pallas_call(
        _hgrn_decode_kernel,
        grid=(b,),
        in_specs=[pl.BlockSpec((None, None, HG_H, HG_DK, HG_DV), lambda i: (layer, i, 0, 0, 0)),
                  colspec, colspec, rowspec, rowspec,
                  pl.BlockSpec((HG_H, HG_DK, 1), lambda i: (0, 0, 0)),
                  pl.BlockSpec((1, MIX_W), lambda i: (0, 0))],
        out_specs=[rowspec, pl.BlockSpec((None, HG_H, HG_DK, HG_DV), lambda i: (i, 0, 0, 0))],
        out_shape=[jax.ShapeDtypeStruct((b, 1, MIX_W), jnp.bfloat16),
                   jax.ShapeDtypeStruct((b, HG_H, HG_DK, HG_DV), jnp.float32)],
        compiler_params=_cparams(("parallel",)),
        name="hgrn_decode",
    )(state, hq_col, hf_col, hi_row, hg_row, lb.reshape(HG_H, HG_DK, 1), norm_w.reshape(1, MIX_W))


def masked_probs(s, mask):
    p = jax.nn.softmax(jnp.where(mask, s, NEG), axis=-1)
    return jnp.where(mask, p, 0.0)


def map_query_chunks(fn, *xs):
    n_q = xs[0].shape[0]
    qc = math.gcd(n_q, Q_CHUNK)
    n = n_q // qc
    blocks = tuple(a.reshape((n, qc) + a.shape[1:]) for a in xs)
    out = lax.map(lambda a: fn(*a), blocks)
    return out.reshape((n_q,) + out.shape[2:])


def gather_pages(pool, pages):
    rows = pool[pages]
    return rows.reshape((-1,) + pool.shape[2:])


def nsa_compress(x, pos_emb, w1, w2):
    n_blk = (x.shape[0] - CMP_LEN) // CMP_STRIDE + 1
    idx = np.arange(n_blk)[:, None] * CMP_STRIDE + np.arange(CMP_LEN)[None, :]
    blk = x[idx] + pos_emb[None, :, None, :]
    blk = blk.transpose(0, 2, 1, 3).reshape(n_blk, x.shape[1], CMP_LEN * HEAD_DIM)
    return jax.nn.silu(blk @ w1) @ w2


def nsa_seq(q, q_pos, kc, vc, ks, vs, cmp_pos, cmp_w1, cmp_w2):
    n_q, seq_len = q.shape[0], kc.shape[0]
    rep = NSA_H // NSA_KVH
    scale = HEAD_DIM ** -0.5
    qg = q.reshape(n_q, NSA_KVH, rep, HEAD_DIM)
    k_cmp = nsa_compress(kc, cmp_pos[0], cmp_w1[0], cmp_w2[0])
    v_cmp = nsa_compress(vc, cmp_pos[1], cmp_w1[1], cmp_w2[1])
    n_cmp = k_cmp.shape[0]
    cmp_start = np.arange(n_cmp, dtype=np.int32) * CMP_STRIDE
    s = jnp.einsum('qgrd,ngd->qgrn', qg, k_cmp, preferred_element_type=jnp.float32) * scale
    cmask = (cmp_start + CMP_LEN - 1)[None, :] <= q_pos[:, None]
    p_cmp = masked_probs(s, cmask[:, None, None, :])
    o_cmp = jnp.einsum('qgrn,ngd->qgrd', p_cmp.astype(v_cmp.dtype), v_cmp)
    n_sel = -(-seq_len // SEL_BLOCK)
    sel_start = np.arange(n_sel, dtype=np.int32) * SEL_BLOCK
    overlap = ((cmp_start[:, None] < sel_start[None, :] + SEL_BLOCK)
               & (cmp_start[:, None] + CMP_LEN > sel_start[None, :])).astype(np.float32)
    imp = jnp.einsum('qgrn,nj->qgj', p_cmp, overlap, precision=lax.Precision.HIGHEST)
    cur = q_pos // SEL_BLOCK
    blk = np.arange(n_sel, dtype=np.int32)[None, :]
    valid = blk <= cur[:, None]
    forced = (blk == 0) | (blk == cur[:, None]) | (blk == cur[:, None] - 1)
    score = jnp.where(valid[:, None, :],
                      imp + jnp.where(forced, FORCE_BONUS, 0.0)[:, None, :], NEG)
    n_top = min(SEL_TOPK, n_sel)
    _, idx = lax.top_k(score, n_top)
    ok = np.arange(n_top, dtype=np.int32)[None, :] < jnp.minimum(cur + 1, n_top)[:, None]
    pad = n_sel * SEL_BLOCK - seq_len
    ksb = jnp.pad(ks, ((0, pad), (0, 0), (0, 0))).reshape(
        n_sel, SEL_BLOCK, NSA_KVH, HEAD_DIM).transpose(2, 0, 1, 3)
    vsb = jnp.pad(vs, ((0, pad), (0, 0), (0, 0))).reshape(
        n_sel, SEL_BLOCK, NSA_KVH, HEAD_DIM).transpose(2, 0, 1, 3)
    g_ix = np.arange(NSA_KVH)[None, :, None]
    offs = np.arange(SEL_BLOCK, dtype=np.int32)

    def sel_chunk(q_c, pos_c, idx_c, ok_c):
        n = q_c.shape[0]
        k_g = ksb[g_ix, idx_c]
        v_g = vsb[g_ix, idx_c]
        k_pos = idx_c[..., None] * SEL_BLOCK + offs
        m = (k_pos <= pos_c[:, None, None, None]) & ok_c[:, None, :, None]
        s_c = jnp.einsum('qgrd,qgkbd->qgrkb', q_c, k_g, preferred_element_type=jnp.float32) * scale
        p_c = masked_probs(s_c.reshape(n, NSA_KVH, rep, -1), m.reshape(n, NSA_KVH, 1, -1))
        return jnp.einsum('qgrkb,qgkbd->qgrd', p_c.reshape(s_c.shape).astype(v_g.dtype), v_g)

    o_slc = map_query_chunks(sel_chunk, qg, q_pos, idx, ok)
    return o_cmp.reshape(n_q, NSA_H, HEAD_DIM), o_slc.reshape(n_q, NSA_H, HEAD_DIM)


def window_banded(q, k, v):
    b, t = q.shape[:2]
    rep = NSA_H // NSA_KVH
    qb = math.gcd(t, WIN_QBLOCK)
    nb = t // qb
    span = WINDOW + qb
    idx = np.arange(nb)[:, None] * qb + np.arange(span)[None, :]
    k_pos = idx - WINDOW
    q_pos = np.arange(t).reshape(nb, qb)
    d = q_pos[:, :, None] - k_pos[:, None, :]
    mask = (d >= 0) & (d <= WINDOW) & (k_pos[:, None, :] >= 0)
    kp = jnp.pad(k, ((0, 0), (WINDOW, 0), (0, 0), (0, 0)))[:, idx]
    vp = jnp.pad(v, ((0, 0), (WINDOW, 0), (0, 0), (0, 0)))[:, idx]
    qg = q.reshape(b, nb, qb, NSA_KVH, rep, HEAD_DIM)
    s = jnp.einsum('bnqgrd,bnkgd->bnqgrk', qg, kp, preferred_element_type=jnp.float32) * (HEAD_DIM ** -0.5)
    p = masked_probs(s, mask[None, :, :, None, None, :])
    o = jnp.einsum('bnqgrk,bnkgd->bnqgrd', p.astype(vp.dtype), vp)
    return o.reshape(b, t, NSA_H, HEAD_DIM)


def window_direct(q, q_pos, k, v, k_pos):
    b, n_q = q.shape[:2]
    rep = NSA_H // NSA_KVH
    qg = q.reshape(b, n_q, NSA_KVH, rep, HEAD_DIM)
    s = jnp.einsum('bqgrd,bkgd->bqgrk', qg, k, preferred_element_type=jnp.float32) * (HEAD_DIM ** -0.5)
    d = q_pos[:, None] - k_pos[None, :]
    mask = (d >= 0) & (d <= WINDOW)
    p = masked_probs(s, mask[None, :, None, None, :])
    o = jnp.einsum('bqgrk,bkgd->bqgrd', p.astype(v.dtype), v)
    return o.reshape(b, n_q, NSA_H, HEAD_DIM)


def nsa_combine(gate_logits, o_cmp, o_slc, o_win):
    b, t = o_cmp.shape[:2]
    g = jax.nn.sigmoid(gate_logits.astype(jnp.float32)).reshape(b, t, 3, NSA_H, 1)
    o = g[:, :, 0] * o_cmp + g[:, :, 1] * o_slc + g[:, :, 2] * o_win
    return o.reshape(b, t, MIX_W).astype(o_cmp.dtype)


def moba_seq(q, q_pos, k, v):
    n_q, seq_len = q.shape[0], k.shape[0]
    scale = HEAD_DIM ** -0.5
    nb = -(-seq_len // MOBA_BLOCK)
    pad = nb * MOBA_BLOCK - seq_len
    kb = jnp.pad(k, ((0, pad), (0, 0), (0, 0))).reshape(nb, MOBA_BLOCK, MOBA_H, HEAD_DIM)
    vb = jnp.pad(v, ((0, pad), (0, 0), (0, 0))).reshape(nb, MOBA_BLOCK, MOBA_H, HEAD_DIM)
    k_mean = jnp.mean(kb.astype(jnp.float32), axis=1)
    cur = q_pos // MOBA_BLOCK
    gate = jnp.einsum('qhd,nhd->qhn', q.astype(jnp.float32), k_mean, precision=lax.Precision.HIGHEST)
    past = np.arange(nb, dtype=np.int32)[None, :] < cur[:, None]
    gate = jnp.where(past[:, None, :], gate, NEG)
    n_top = min(MOBA_TOPK, nb)
    _, idx = lax.top_k(gate, n_top)
    ok = np.arange(n_top, dtype=np.int32)[None, :] < jnp.minimum(cur, n_top)[:, None]
    idx = jnp.concatenate([idx, jnp.broadcast_to(cur[:, None, None], (n_q, MOBA_H, 1))], axis=-1)
    ok = jnp.concatenate([ok, jnp.ones((n_q, 1), dtype=bool)], axis=-1)
    kbh = kb.transpose(2, 0, 1, 3)
    vbh = vb.transpose(2, 0, 1, 3)
    h_ix = np.arange(MOBA_H)[None, :, None]
    offs = np.arange(MOBA_BLOCK, dtype=np.int32)

    def chunk(q_c, pos_c, idx_c, ok_c):
        n = q_c.shape[0]
        k_g = kbh[h_ix, idx_c]
        v_g = vbh[h_ix, idx_c]
        k_pos = idx_c[..., None] * MOBA_BLOCK + offs
        m = (k_pos <= pos_c[:, None, None, None]) & ok_c[:, None, :, None]
        s_c = jnp.einsum('qhd,qhkbd->qhkb', q_c, k_g, preferred_element_type=jnp.float32) * scale
        p_c = masked_probs(s_c.reshape(n, MOBA_H, -1), m.reshape(n, MOBA_H, -1))
        return jnp.einsum('qhkb,qhkbd->qhd', p_c.reshape(s_c.shape).astype(v_g.dtype), v_g)

    return map_query_chunks(chunk, q, q_pos, idx, ok)


def hgrn_features(hq, hf, hi, lb):
    b, t, _ = hq.shape
    q = jax.nn.silu(hq.astype(jnp.float32)).reshape(b, t, HG_H, HG_DK)
    z = hf.astype(jnp.float32).reshape(b, t, HG_H, HG_DK)
    lbh = lb.astype(jnp.float32).reshape(HG_H, HG_DK)
    f = lbh + (1.0 - lbh) * jax.nn.sigmoid(z)
    logf = jnp.log(jnp.maximum(f, F_FLOOR))
    k = (1.0 - lbh) * jax.nn.sigmoid(-z)
    v = hi.astype(jnp.float32).reshape(b, t, HG_H, HG_DV)
    return q, k, logf, v


def hgrn_chunked(q, k, logf, v, s0):
    b, t, h, _ = q.shape
    c = math.gcd(t, HG_CHUNK)
    n = t // c
    tri = np.tril(np.ones((c, c), dtype=bool))[:, :, None]

    def blocks(a):
        return a.reshape(b, n, c, h, a.shape[-1]).transpose(1, 0, 3, 2, 4)

    def step(s, inp):
        q_c, k_c, l_c, v_c = inp
        cum = jnp.cumsum(l_c, axis=2)
        diff = cum[:, :, :, None, :] - cum[:, :, None, :, :]
        decay = jnp.where(tri, jnp.exp(jnp.where(tri, diff, 0.0)), 0.0)
        a = jnp.einsum('bhtd,bhsd,bhtsd->bhts', q_c, k_c, decay)
        o = jnp.einsum('bhts,bhse->bhte', a, v_c) + jnp.einsum('bhtd,bhde->bhte', q_c * jnp.exp(cum), s)
        last = cum[:, :, -1]
        s = jnp.exp(last)[..., None] * s + jnp.einsum(
            'bhsd,bhse->bhde', k_c * jnp.exp(last[:, :, None] - cum), v_c)
        return s, o

    s, o = lax.scan(step, s0, (blocks(q), blocks(k), blocks(logf), blocks(v)))
    return o.transpose(1, 0, 3, 2, 4).reshape(b, t, h, v.shape[-1]), s


def hgrn_recurrent(q, k, logf, v, s0):
    def step(s, inp):
        q_t, k_t, l_t, v_t = inp
        s = jnp.exp(l_t)[..., None] * s + k_t[..., None] * v_t[..., None, :]
        return s, jnp.einsum('bhd,bhde->bhe', q_t, s)

    sw = lambda a: jnp.swapaxes(a, 0, 1)
    s, o = lax.scan(step, s0, (sw(q), sw(k), sw(logf), sw(v)))
    return sw(o), s


def hgrn_output(o, hg, norm_w):
    b, t = o.shape[:2]
    o = o * lax.rsqrt(jnp.mean(o * o, axis=-1, keepdims=True) + EPS)
    o = o * norm_w.astype(jnp.float32).reshape(HG_H, HG_DV)
    g = jax.nn.silu(hg.astype(jnp.float32)).reshape(b, t, HG_H, HG_DV)
    return (o * g).reshape(b, t, MIX_W).astype(hg.dtype)


def split_z(z, b, t):
    z = z.reshape(b, t, D_IN_PAD)
    sl = lambda off, n: z[:, :, off:off + n]
    return (sl(OFF_NQ, N_NQ), sl(OFF_NKV, N_NKV), sl(OFF_NGATE, N_NGATE), sl(OFF_MQKV, N_MQKV),
            sl(OFF_HQ, N_HG), sl(OFF_HF, N_HG), sl(OFF_HI, N_HG), sl(OFF_HGATE, N_HG))


def finish_layer(x2, z, o_nsa, o_hg, o_moba, lw):
    mixed = branch_merge((o_nsa, o_hg, o_moba), lw["w_branch"], z)
    x2 = matmul(mixed, lw["w_out"], epilogue="residual", residual=x2)
    h2 = rmsnorm_rows(x2, lw["norm2_w"], jnp.bfloat16)
    u2 = matmul(h2, lw["w_up"], epilogue="relu2", out_dtype=jnp.bfloat16)
    return matmul(u2, lw["w_down"], epilogue="residual", residual=x2)


def prompt_layer(x, lb, lw):
    b, t, _ = x.shape
    x2 = x.reshape(b * t, D_MODEL)
    h = rmsnorm_rows(x2, lw["norm1_w"], jnp.bfloat16)
    z = matmul(h, lw["w_in"])
    ckv = nsa_compress_prompt(z, b, t, lw["cw"])
    o_nsa = nsa_prompt(z, ckv, b, t)
    o_moba = moba_prompt(z, b, t)
    o_hg, s_fin = hgrn_prompt(z, lb, lw["hg_norm_w"], b, t)
    x2 = finish_layer(x2, z, o_nsa, o_hg, o_moba, lw)
    return (x2.reshape(b, t, D_MODEL),) + kv_cache_rows(z, b, t) + (s_fin,)


def kv_cache_rows(z, b, t):
    z3 = z.reshape(b, t, D_IN_PAD)
    gw = 2 * NSA_KVH * HEAD_DIM
    kv = lambda kind: z3[:, :, OFF_NKV + kind * gw:OFF_NKV + (kind + 1) * gw].reshape(b, t, 2, NSA_KVH, HEAD_DIM)
    mkv = z3[:, :, OFF_MQKV + MIX_W:OFF_MQKV + 3 * MIX_W].reshape(b, t, 2, MOBA_H, HEAD_DIM)
    return kv(0), kv(1), mkv, kv(2)[:, max(t - WINDOW, 0):]


def sample_layer(x, layer, caches, page_table, lb, lw):
    b, t, _ = x.shape
    assert t == 1, "the decode kernels take one new token per sequence"
    x2 = x.reshape(b * t, D_MODEL)
    h = rmsnorm_rows(x2, lw["norm1_w"], jnp.bfloat16)
    z = matmul(h, lw["w_in"])
    o_nsa, o_hg, o_moba, s_new = sample_mixers(z, layer, caches, page_table, lb, lw)
    x2 = finish_layer(x2, z, o_nsa, o_hg, o_moba, lw)
    c_cmp, c_slc, c_moba, win_new = kv_cache_rows(z, b, t)
    win_out = jnp.concatenate([caches[3][layer][:, t:], win_new], axis=1)
    return (x2.reshape(b, t, D_MODEL), c_cmp, c_slc, c_moba, win_out, s_new)


def sample_mixers(z, layer, caches, page_table, lb, lw):
    cache_cmp, cache_slc, cache_moba, cache_win, state = caches
    b = z.shape[0]
    n_pages = page_table.shape[1]
    pos = n_pages * PAGE_SIZE
    pt_flat = page_table.reshape(-1)
    col = lambda off, n: z[:, off:off + n]
    q_s = col(OFF_NQ, N_NQ).reshape(b, NSA_H, HEAD_DIM)
    kv_new = col(OFF_NKV, N_NKV).reshape(b, 1, N_NKV)
    gate_cols = col(OFF_NGATE, N_NGATE).reshape(b, 3, NSA_H, 1)
    o_cmp, sel = nsa_cmp_decode(cache_cmp, layer, pt_flat, q_s, lw["cw"], n_pages)
    o_slc = nsa_slc_decode(cache_slc, layer, pt_flat, q_s, kv_new, sel, n_pages)
    o_nsa = nsa_win_decode(cache_win, layer, q_s, kv_new, o_cmp, o_slc, gate_cols, pos)
    q_m = col(OFF_MQKV, MIX_W).reshape(b, MOBA_H, HEAD_DIM)
    k_new = col(OFF_MQKV + MIX_W, MIX_W).reshape(b, 1, MIX_W)
    v_new = col(OFF_MQKV + 2 * MIX_W, MIX_W).reshape(b, 1, MIX_W)
    o_moba = moba_decode(cache_moba, layer, pt_flat, q_m, k_new, v_new, n_pages)
    o_hg, s_new = hgrn_decode(state, layer, col(OFF_HQ, N_HG).reshape(b, HG_H, HG_DK, 1),
                              col(OFF_HF, N_HG).reshape(b, HG_H, HG_DK, 1), col(OFF_HI, N_HG).reshape(b, 1, MIX_W),
                              col(OFF_HGATE, N_HG).reshape(b, 1, MIX_W), lb, lw["hg_norm_w"])
    return o_nsa.reshape(b, MIX_W), o_hg.reshape(b, MIX_W), o_moba.reshape(b, MIX_W), s_new


def prep_layer_weights(i, norm1_w, norm2_w, w_in, nsa_cmp_pos, nsa_cmp_w1, nsa_cmp_w2, hgrn_norm_w,
                       w_branch, w_out, w_up, w_down):
    wi = w_in[i]
    gate_lo = N_NQ + N_NKV
    gate_hi = gate_lo + N_NGATE
    pad = jnp.zeros((D_MODEL, D_IN_PAD - wi.shape[1]), wi.dtype)
    wi = jnp.concatenate([wi[:, :gate_lo], wi[:, gate_hi:], wi[:, gate_lo:gate_hi], pad], axis=1)
    bf = lambda a: a.astype(jnp.bfloat16)
    return dict(norm1_w=norm1_w[i], norm2_w=norm2_w[i], w_in=bf(wi),
                cw=prep_compress_weights(nsa_cmp_pos[i], nsa_cmp_w1[i], nsa_cmp_w2[i]), hg_norm_w=hgrn_norm_w[i],
                w_branch=bf(w_branch[i]), w_out=bf(w_out[i]), w_up=bf(w_up[i]), w_down=bf(w_down[i]))


def kernel(x_prompt, x_sample, cache_nsa_cmp, cache_nsa_slc, cache_moba, cache_nsa_win, state_hgrn,
           page_table, norm1_w, norm2_w, w_in, nsa_cmp_pos, nsa_cmp_w1, nsa_cmp_w2, hgrn_lb_logits,
           hgrn_norm_w, w_branch, w_out, w_up, w_down, final_norm_w):
    sm = jax.nn.softmax(hgrn_lb_logits.astype(jnp.float32), axis=0)
    lbs = jnp.cumsum(sm, axis=0) - sm[0:1]
    xp, xs = x_prompt, x_sample
    outs_p, outs_s = [], []
    for i in range(DEPTH):
        lw = prep_layer_weights(i, norm1_w, norm2_w, w_in, nsa_cmp_pos, nsa_cmp_w1, nsa_cmp_w2,
                                hgrn_norm_w, w_branch, w_out, w_up, w_down)
        xp, *st_p = prompt_layer(xp, lbs[i], lw)
        xs, *st_s = sample_layer(xs, i, (cache_nsa_cmp, cache_nsa_slc, cache_moba, cache_nsa_win, state_hgrn),
                                 page_table, lbs[i], lw)
        outs_p.append(st_p)
        outs_s.append(st_s)
    y_prompt = rmsnorm_rows(xp.reshape(-1, D_MODEL), final_norm_w, jnp.float32).reshape(xp.shape)
    y_sample = rmsnorm_rows(xs.reshape(-1, D_MODEL), final_norm_w, jnp.float32).reshape(xs.shape)
    stack = lambda outs, j: jnp.stack([o[j] for o in outs])
    return (y_prompt, y_sample) + tuple(stack(outs_p, j) for j in range(5)) + tuple(
        stack(outs_s, j) for j in range(5))
```

```python
import functools
import math

import jax
import jax.numpy as jnp
import numpy as np
from jax import lax
from jax.experimental import pallas as pl
from jax.experimental.pallas import tpu as pltpu

D_MODEL = 2048
DEPTH = 2
PAGE_SIZE = 128
HEAD_DIM = 128
MIX_W = D_MODEL // 2
N_BRANCH = 3
NSA_H = MIX_W // HEAD_DIM
NSA_KVH = NSA_H // 4
CMP_LEN = 32
CMP_STRIDE = 16
SEL_BLOCK = 64
SEL_TOPK = 16
WINDOW = 512
WIN_QBLOCK = 128
FORCE_BONUS = 1e4
MOBA_H = MIX_W // HEAD_DIM
MOBA_BLOCK = 256
MOBA_TOPK = 3
HG_H = MIX_W // HEAD_DIM
HG_DK = 128
HG_DV = MIX_W // HG_H
HG_CHUNK = 64
D_FF = 4 * D_MODEL
Q_CHUNK = 32
EPS = 1e-6
F_FLOOR = 1e-30
NEG = -1e30

N_NQ = NSA_H * HEAD_DIM
N_NKV = 6 * NSA_KVH * HEAD_DIM
N_NGATE = 3 * NSA_H
N_MQKV = 3 * MOBA_H * HEAD_DIM
N_HG = HG_H * HG_DK
N_MG = N_BRANCH * D_MODEL
OFF_NQ = 0
OFF_NKV = OFF_NQ + N_NQ
OFF_MQKV = OFF_NKV + N_NKV
OFF_HQ = OFF_MQKV + N_MQKV
OFF_HF = OFF_HQ + N_HG
OFF_HI = OFF_HF + N_HG
OFF_HGATE = OFF_HI + N_HG
OFF_MG = OFF_HGATE + N_HG
OFF_NGATE = OFF_MG + N_MG
D_IN_PAD = 16384

VMEM_LIMIT_BYTES = 48 * 1024 * 1024


def _cparams(sem):
    return pltpu.CompilerParams(dimension_semantics=sem, vmem_limit_bytes=VMEM_LIMIT_BYTES)


def _rmsnorm_kernel(x_ref, w_ref, o_ref):
    x = x_ref[...]
    y = x * lax.rsqrt(jnp.mean(x * x, axis=-1, keepdims=True) + EPS)
    o_ref[...] = (y * w_ref[...]).astype(o_ref.dtype)


def rmsnorm_rows(x, w, out_dtype):
    m, d = x.shape
    tm = min(m, 512)
    return pl.pallas_call(
        _rmsnorm_kernel,
        grid=(m // tm,),
        in_specs=[pl.BlockSpec((tm, d), lambda i: (i, 0)), pl.BlockSpec((1, d), lambda i: (0, 0))],
        out_specs=pl.BlockSpec((tm, d), lambda i: (i, 0)),
        out_shape=jax.ShapeDtypeStruct((m, d), out_dtype),
        compiler_params=_cparams(("parallel",)),
        name="rmsnorm",
    )(x, w.reshape(1, d))


def _mm_kernel(*refs, epilogue, nk):
    if epilogue == "residual":
        a_ref, w_ref, r_ref, o_ref = refs[:4]
        rest = refs[4:]
    else:
        a_ref, w_ref, o_ref = refs[:3]
        r_ref = None
        rest = refs[3:]
    part = jnp.dot(a_ref[...], w_ref[...], preferred_element_type=jnp.float32)

    def finish(acc):
        if epilogue == "residual":
            o_ref[...] = r_ref[...] + acc
        elif epilogue == "relu2":
            u = jnp.maximum(acc, 0.0)
            o_ref[...] = (u * u).astype(o_ref.dtype)
        else:
            o_ref[...] = acc.astype(o_ref.dtype)

    if nk == 1:
        finish(part)
    else:
        acc_ref = rest[0]
        k = pl.program_id(2)

        @pl.when(k == 0)
        def _():
            acc_ref[...] = part

        @pl.when(k > 0)
        def _():
            acc_ref[...] += part

        @pl.when(k == nk - 1)
        def _():
            finish(acc_ref[...])


def matmul(a, w, *, epilogue="none", residual=None, out_dtype=jnp.float32, tm=1024, tn=1024, tk=2048):
    m, kdim = a.shape
    n = w.shape[1]
    tm, tn, tk = min(tm, m), min(tn, n), min(tk, kdim)
    nk = kdim // tk
    in_specs = [pl.BlockSpec((tm, tk), lambda j, i, k: (i, k)),
                pl.BlockSpec((tk, tn), lambda j, i, k: (k, j))]
    args = [a, w]
    if epilogue == "residual":
        in_specs.append(pl.BlockSpec((tm, tn), lambda j, i, k: (i, j)))
        args.append(residual)
    scratch = [pltpu.VMEM((tm, tn), jnp.float32)] if nk > 1 else []
    return pl.pallas_call(
        functools.partial(_mm_kernel, epilogue=epilogue, nk=nk),
        grid=(n // tn, m // tm, nk),
        in_specs=in_specs,
        out_specs=pl.BlockSpec((tm, tn), lambda j, i, k: (i, j)),
        out_shape=jax.ShapeDtypeStruct((m, n), out_dtype),
        scratch_shapes=scratch,
        compiler_params=_cparams(("parallel", "parallel", "arbitrary")),
        name="matmul_" + epilogue,
    )(*args)


def _merge_kernel(b0_ref, b1_ref, b2_ref, wb_ref, g0_ref, g1_ref, g2_ref, o_ref):
    acc = None
    for n, (b_ref, g_ref) in enumerate(((b0_ref, g0_ref), (b1_ref, g1_ref), (b2_ref, g2_ref))):
        proj = jnp.dot(b_ref[...], wb_ref[n], preferred_element_type=jnp.float32)
        term = jax.nn.sigmoid(g_ref[...]) * proj
        acc = term if acc is None else acc + term
    o_ref[...] = acc.astype(o_ref.dtype)


def branch_merge(branches, wb, z, *, tm=512, tn=512):
    m = branches[0].shape[0]
    tm = min(tm, m)
    gate_specs = [
        pl.BlockSpec((tm, tn), lambda j, i, n=n: (i, (OFF_MG + n * D_MODEL) // tn + j)) for n in range(N_BRANCH)]
    return pl.pallas_call(
        _merge_kernel,
        grid=(D_MODEL // tn, m // tm),
        in_specs=[pl.BlockSpec((tm, MIX_W), lambda j, i: (i, 0))] * N_BRANCH
        + [pl.BlockSpec((N_BRANCH, MIX_W, tn), lambda j, i: (0, 0, j))] + gate_specs,
        out_specs=pl.BlockSpec((tm, tn), lambda j, i: (i, j)),
        out_shape=jax.ShapeDtypeStruct((m, D_MODEL), jnp.bfloat16),
        compiler_params=_cparams(("parallel", "parallel")),
        name="branch_merge",
    )(*branches, wb, z, z, z)


SCALE = HEAD_DIM ** -0.5
NSA_REP = NSA_H // NSA_KVH
CMP_PER_PAGE = PAGE_SIZE // CMP_STRIDE
CMP_HALF = CMP_LEN // CMP_STRIDE * 0 + CMP_STRIDE * HEAD_DIM


def _bf(x):
    return x.astype(jnp.bfloat16)


def _split3(x):
    hi = _bf(x)
    r1 = x - hi.astype(jnp.float32)
    mid = _bf(r1)
    lo = _bf(r1 - mid.astype(jnp.float32))
    return hi, mid, lo


_NT = (((1,), (1,)), ((), ()))


def _dot(a, b):
    return jnp.dot(a, b, preferred_element_type=jnp.float32)


def _dot_nt(a, b):
    return lax.dot_general(a, b, _NT, preferred_element_type=jnp.float32)


def _dot_f32_lhs(a, b_exact):
    return sum(_dot(p, b_exact) for p in _split3(a))


def _dot_f32_rhs(a_exact, b):
    return sum(_dot(a_exact, p) for p in _split3(b))


def _dot_nt_f32(a, b):
    a1, a2, a3 = _split3(a)
    b1, b2, b3 = _split3(b)
    return (_dot_nt(a1, b1) + (_dot_nt(a1, b2) + _dot_nt(a2, b1))
            + (_dot_nt(a1, b3) + _dot_nt(a2, b2) + _dot_nt(a3, b1)))


def _masked_softmax(s, mask):
    s = jnp.where(mask, s, NEG)
    m = jnp.max(s, axis=-1, keepdims=True)
    e = jnp.where(mask, jnp.exp(s - m), 0.0)
    l = jnp.sum(e, axis=-1, keepdims=True)
    return e / jnp.where(l > 0.0, l, 1.0)


def _masked_softmax_rows(s, mask):
    s = jnp.where(mask, s, NEG)
    m = jnp.max(s, axis=0, keepdims=True)
    e = jnp.where(mask, jnp.exp(s - m), 0.0)
    l = jnp.sum(e, axis=0, keepdims=True)
    return e / jnp.where(l > 0.0, l, 1.0)


def _rank_lanes(score, n):
    lane = lax.broadcasted_iota(jnp.int32, score.shape, 1)
    rank = jnp.zeros(score.shape, jnp.float32)
    for i in range(n):
        col = score[:, i:i + 1]
        ahead = jnp.where(col == score, jnp.where(lane > i, 1.0, 0.0), jnp.where(col > score, 1.0, 0.0))
        rank = rank + ahead
    return rank


def _rank_sublanes(score, n):
    row = lax.broadcasted_iota(jnp.int32, score.shape, 0)
    rank = jnp.zeros(score.shape, jnp.float32)
    for i in range(n):
        r = score[i:i + 1, :]
        ahead = jnp.where(r == score, jnp.where(row > i, 1.0, 0.0), jnp.where(r > score, 1.0, 0.0))
        rank = rank + ahead
    return rank


def _compress_tokens(x_bf, w1ab_ref, w2_ref, posrows_ref):
    n = x_bf.shape[0]
    w1ab = w1ab_ref[...]
    pre = _dot(x_bf, w1ab)
    pb = _dot(posrows_ref[...], w1ab)
    posbias = pb[0:1, :HEAD_DIM] + pb[1:2, HEAD_DIM:]
    nxt = pltpu.roll(pre[:, HEAD_DIM:], n - 1, 0)
    hid = pre[:, :HEAD_DIM] + nxt + posbias
    return _dot(_bf(jax.nn.silu(hid)), w2_ref[...])


def prep_compress_weights(cmp_pos, cmp_w1, cmp_w2):
    half = CMP_STRIDE * HEAD_DIM
    w1ab = jnp.concatenate([cmp_w1[:, :half], cmp_w1[:, half:]], axis=-1)
    pos2 = cmp_pos.reshape(2, 2, half)
    posrows = jnp.concatenate([pos2, jnp.zeros((2, 6, half), cmp_pos.dtype)], axis=1)
    return _bf(w1ab), _bf(cmp_w2), _bf(posrows)


def _cmp_prompt_kernel(x_ref, w1ab_ref, w2_ref, posrows_ref, o_ref):
    n_chunks = x_ref.shape[0] // CMP_STRIDE
    xc = jnp.concatenate(
        [_bf(x_ref[pl.ds(j, n_chunks, stride=CMP_STRIDE), :]) for j in range(CMP_STRIDE)], axis=-1)
    o_ref[...] = _compress_tokens(xc, w1ab_ref, w2_ref, posrows_ref)


def nsa_compress_prompt(z, b, t, cw):
    w1ab, w2, posrows = cw
    n_chunks = t // CMP_STRIDE
    kind_spec = lambda a: pl.BlockSpec((None,) + a.shape[1:], lambda i, c: (c // NSA_KVH, 0, 0))
    return pl.pallas_call(
        _cmp_prompt_kernel,
        grid=(b, 2 * NSA_KVH),
        in_specs=[pl.BlockSpec((t, HEAD_DIM), lambda i, c: (i, OFF_NKV // HEAD_DIM + c)),
                  kind_spec(w1ab), kind_spec(w2), kind_spec(posrows)],
        out_specs=pl.BlockSpec((None, None, n_chunks, HEAD_DIM), lambda i, c: (i, c, 0, 0)),
        out_shape=jax.ShapeDtypeStruct((b, 2 * NSA_KVH, n_chunks, HEAD_DIM), jnp.float32),
        compiler_params=_cparams(("parallel", "parallel")),
        name="nsa_compress_prompt",
    )(z, w1ab, w2, posrows)


NSA_TQ = 128
NSA_KEY_SPANS = 4


def _nsa_prompt_kernel(q_ref, ckv_ref, ks_ref, vs_ref, kw_ref, vw_ref, gate_ref, ovl_ref, exp_ref, o_ref,
                       ks_bf, vs_bf, kw_bf, vw_bf, oslc_sc, *, t):
    qt = pl.program_id(1)
    q0 = pl.multiple_of(qt * NSA_TQ, NSA_TQ)
    n_cmp = ckv_ref.shape[1]
    n_sel = t // SEL_BLOCK
    span = WINDOW + NSA_TQ
    nqt = t // NSA_TQ
    n_span = min(NSA_KEY_SPANS, nqt)

    @pl.when(qt == 0)
    def _():
        ks_bf[...] = _bf(ks_ref[...])
        vs_bf[...] = _bf(vs_ref[...])
        kw_bf[...] = _bf(kw_ref[...])
        vw_bf[...] = _bf(vw_ref[...])

    qpos = q0 + lax.broadcasted_iota(jnp.int32, (NSA_TQ, 1), 0)
    gates = jax.nn.sigmoid(gate_ref[...])
    lane = lax.broadcasted_iota(jnp.int32, (NSA_TQ, 128), 1)
    lane_c = lax.broadcasted_iota(jnp.int32, (NSA_TQ, n_cmp), 1)
    cmask = (lane_c * CMP_STRIDE + (CMP_LEN - 1) <= qpos) & (lane_c < n_cmp - 1)
    cur = qpos // SEL_BLOCK
    forced = (lane == 0) | (lane == cur) | (lane == cur - 1)
    wstart =pl.multiple_of(jnp.maximum(q0 - WINDOW, 0), NSA_TQ)
    wd = qpos - (wstart + lax.broadcasted_iota(jnp.int32, (NSA_TQ, span), 1))
    wmask = (wd >= 0) & (wd <= WINDOW)

    def attend(qs, k_bf, v_bf, mask):
        nk = k_bf.shape[0]
        s = _dot_nt(qs, k_bf).reshape(NSA_REP, NSA_TQ, nk) * SCALE
        p = _masked_softmax(s, mask[None])
        return p, _dot(_bf(p.reshape(NSA_REP * NSA_TQ, nk)), v_bf)

    for g in range(NSA_KVH):
        qs = _bf(jnp.concatenate(
            [q_ref[:, (g * NSA_REP + r) * HEAD_DIM:(g * NSA_REP + r + 1) * HEAD_DIM] for r in range(NSA_REP)],
            axis=0))
        p_cmp, o_cmp = attend(qs, _bf(ckv_ref[g]), _bf(ckv_ref[NSA_KVH + g]), cmask)
        psum = p_cmp[0] + p_cmp[1] + p_cmp[2] + p_cmp[3]
        imp = _dot_f32_lhs(psum, ovl_ref[...])
        score = jnp.where(lane <= cur, imp + jnp.where(forced, FORCE_BONUS, 0.0), NEG)
        rank = _rank_lanes(score, n_sel)
        n_ok = jnp.minimum(cur + 1, min(SEL_TOPK, n_sel)).astype(jnp.float32)
        sel = jnp.where(rank < n_ok, 1.0, 0.0)
        sel_b = _bf(sel)
        gsl = slice(g * HEAD_DIM, (g + 1) * HEAD_DIM)

        def selected(nk, sel_b=sel_b, qs=qs, gsl=gsl):
            keysel = _dot(sel_b, exp_ref[:, 0:nk])
            kpos = lax.broadcasted_iota(jnp.int32, (NSA_TQ, nk), 1)
            smask = jnp.where(kpos <= qpos, keysel, 0.0) > 0.5
            oslc_sc[...] = attend(qs, ks_bf[0:nk, gsl], vs_bf[0:nk, gsl], smask)[1]

        for j in range(n_span):
            pl.when(qt // (nqt // n_span) == j)(functools.partial(selected, (j + 1) * (t // n_span)))
        o_slc = oslc_sc[...]
        _, o_win = attend(qs, kw_bf[pl.ds(wstart, span), gsl], vw_bf[pl.ds(wstart, span), gsl], wmask)
        for r in range(NSA_REP):
            h = g * NSA_REP + r
            rows = slice(r * NSA_TQ, (r + 1) * NSA_TQ)
            o = (gates[:, h:h + 1] * o_cmp[rows] + gates[:, NSA_H + h:NSA_H + h + 1] * o_slc[rows]
                 + gates[:, 2 * NSA_H + h:2 * NSA_H + h + 1] * o_win[rows])
            o_ref[:, h * HEAD_DIM:(h + 1) * HEAD_DIM] = o.astype(o_ref.dtype)


def nsa_prompt(z, ckv, b, t):
    n_cmp = ckv.shape[2]
    n_sel = t // SEL_BLOCK
    nqt = t // NSA_TQ
    gw = NSA_KVH * HEAD_DIM
    cmp_start = np.arange(n_cmp) * CMP_STRIDE
    sel_start = np.arange(128) * SEL_BLOCK
    ovl = ((cmp_start[:, None] < sel_start[None, :] + SEL_BLOCK) & (cmp_start[:, None] + CMP_LEN > sel_start[None, :])
           & (np.arange(n_cmp)[:, None] < n_cmp - 1) & (np.arange(128)[None, :] < n_sel))
    ovl = jnp.asarray(ovl, jnp.bfloat16)
    expand = jnp.asarray(np.arange(128)[:, None] == (np.arange(t)[None, :] // SEL_BLOCK), jnp.bfloat16)
    kv_spec = lambda kind: pl.BlockSpec((t, gw), lambda i, j: (i, OFF_NKV // gw + kind))
    return pl.pallas_call(
        functools.partial(_nsa_prompt_kernel, t=t),
        grid=(b, nqt),
        in_specs=[pl.BlockSpec((NSA_TQ, N_NQ), lambda i, j: (i * nqt + j, OFF_NQ // N_NQ)),
                  pl.BlockSpec((None, 2 * NSA_KVH, n_cmp, HEAD_DIM), lambda i, j: (i, 0, 0, 0)),
                  kv_spec(2), kv_spec(3), kv_spec(4), kv_spec(5),
                  pl.BlockSpec((NSA_TQ, 128), lambda i, j: (i * nqt + j, OFF_NGATE // 128)),
                  pl.BlockSpec(ovl.shape, lambda i, j: (0, 0)),
                  pl.BlockSpec(expand.shape, lambda i, j: (0, 0))],
        out_specs=pl.BlockSpec((NSA_TQ, MIX_W), lambda i, j: (i * nqt + j, 0)),
        out_shape=jax.ShapeDtypeStruct((b * t, MIX_W), jnp.bfloat16),
        scratch_shapes=[pltpu.VMEM((t, gw), jnp.bfloat16)] * 4
        + [pltpu.VMEM((NSA_REP * NSA_TQ, HEAD_DIM), jnp.float32)],
        compiler_params=_cparams(("parallel", "arbitrary")),
        name="nsa_prompt",
    )(z, ckv, z, z, z, z, z, ovl, expand)


def _moba_prompt_kernel(q_ref, k_ref, v_ref, exp_ref, o_ref, k_bf, v_bf, km_ref, *, t):
    qt = pl.program_id(2)
    nb = t // MOBA_BLOCK

    @pl.when(qt == 0)
    def _():
        k = k_ref[...]
        k_bf[...] = _bf(k)
        v_bf[...] = _bf(v_ref[...])
        km_ref[...] = jnp.zeros(km_ref.shape, jnp.float32)
        km_ref[0:nb, :] = jnp.mean(k.reshape(nb, MOBA_BLOCK, HEAD_DIM), axis=1)

    q = q_ref[...]
    qpos = qt * MOBA_BLOCK + lax.broadcasted_iota(jnp.int32, (MOBA_BLOCK, 1), 0)
    lane = lax.broadcasted_iota(jnp.int32, (MOBA_BLOCK, 128), 1)
    gate = jnp.where(lane < qt, _dot_nt_f32(q, km_ref[...]), NEG)
    rank = _rank_lanes(gate, nb)
    n_ok = jnp.minimum(qt, min(MOBA_TOPK, nb)).astype(jnp.float32)
    sel = _bf(jnp.where(rank < n_ok, 1.0, 0.0))
    q_bf = _bf(q)

    def attend(nk):
        keysel = _dot(sel, exp_ref[:, 0:nk])
        kpos = lax.broadcasted_iota(jnp.int32, (MOBA_BLOCK, nk), 1)
        own = (kpos // MOBA_BLOCK == qt) & (kpos <= qpos)
        mask = jnp.where(own, 1.0, keysel) > 0.5
        s = _dot_nt(q_bf, k_bf[0:nk, :]) * SCALE
        p = _masked_softmax(s, mask)
        o_ref[...] = _dot(_bf(p), v_bf[0:nk, :]).astype(o_ref.dtype)

    for j in range(nb):
        pl.when(qt == j)(functools.partial(attend, (j + 1) * MOBA_BLOCK))


def moba_prompt(z, b, t):
    nqt = t // MOBA_BLOCK
    expand = jnp.asarray(np.arange(128)[:, None] == (np.arange(t)[None, :] // MOBA_BLOCK), jnp.bfloat16)
    col = lambda part: OFF_MQKV // HEAD_DIM + part * MOBA_H
    return pl.pallas_call(
        functools.partial(_moba_prompt_kernel, t=t),
        grid=(b, MOBA_H, nqt),
        in_specs=[pl.BlockSpec((MOBA_BLOCK, HEAD_DIM), lambda i, h, j: (i * nqt + j, col(0) + h)),
                  pl.BlockSpec((t, HEAD_DIM), lambda i, h, j: (i, col(1) + h)),
                  pl.BlockSpec((t, HEAD_DIM), lambda i, h, j: (i, col(2) + h)),
                  pl.BlockSpec(expand.shape, lambda i, h, j: (0, 0))],
        out_specs=pl.BlockSpec((MOBA_BLOCK, HEAD_DIM), lambda i, h, j: (i * nqt + j, h)),
        out_shape=jax.ShapeDtypeStruct((b * t, MIX_W), jnp.bfloat16),
        scratch_shapes=[pltpu.VMEM((t, HEAD_DIM), jnp.bfloat16), pltpu.VMEM((t, HEAD_DIM), jnp.bfloat16),
                        pltpu.VMEM((128, HEAD_DIM), jnp.float32)],
        compiler_params=_cparams(("parallel", "parallel", "arbitrary")),
        name="moba_prompt",
    )(z, z, z, expand)


HG_KCHUNK = 128
HG_SUB = 16
HG_HEADS_PER_STEP = 2


def _hgrn_gates(hq, hf, lb):
    q = jax.nn.silu(hq)
    f = lb + (1.0 - lb) * jax.nn.sigmoid(hf)
    logf = jnp.log(jnp.maximum(f, F_FLOOR))
    k = (1.0 - lb) * jax.nn.sigmoid(-hf)
    return q, k, logf


def _hgrn_out(o, hg, norm_w):
    o = o * lax.rsqrt(jnp.mean(o * o, axis=-1, keepdims=True) + EPS)
    return o * norm_w * jax.nn.silu(hg)


def _hgrn_prompt_kernel(hq_ref, hf_ref, hi_ref, hg_ref, lb_ref, nw_ref, tri_ref, o_ref, s_ref,
                        st_ref, kcv_sc, *, t):
    c = HG_KCHUNK
    st_ref[...] = jnp.zeros(st_ref.shape, jnp.float32)
    row_s = lax.broadcasted_iota(jnp.int32, (HG_SUB, 1), 0)
    col_c = lax.broadcasted_iota(jnp.int32, (HG_SUB, c), 1)

    def head_chunk(hh, rows):
        lanes = slice(hh * HG_DK, (hh + 1) * HG_DK)
        lb = lb_ref[:, lanes]
        nw = nw_ref[:, lanes]
        k_sc, cum_sc, v_sc = kcv_sc.at[hh, 0], kcv_sc.at[hh, 1], kcv_sc.at[hh, 2]
        q, k, logf = _hgrn_gates(hq_ref[rows, lanes], hf_ref[rows, lanes], lb)
        v = hi_ref[rows, lanes]
        cum = _dot_f32_rhs(tri_ref[...], logf)
        k_sc[...] = k
        cum_sc[...] = cum
        v_sc[...] = v
        v_b = _bf(v)
        st = st_ref[hh]
        o_carry = _dot_nt(_bf(q * jnp.exp(cum)), _bf(st))
        o_parts = []
        for i in range(c // HG_SUB):
            r0 = i * HG_SUB
            q_i = q[r0:r0 + HG_SUB]
            cum_i = cum[r0:r0 + HG_SUB]
            o_i = o_carry[r0:r0 + HG_SUB]
            if i > 0:
                edge = cum[r0 - 1:r0]
                k_e = k * jnp.exp(jnp.minimum(edge - cum, 0.0))
                a = _dot_nt(_bf(q_i * jnp.exp(cum_i - edge)), _bf(k_e))
                o_i = o_i + _dot(_bf(jnp.where(col_c < r0, a, 0.0)), v_b)
            for s in range(r0, r0 + HG_SUB):
                d = jnp.minimum(cum_i - cum_sc[pl.ds(s, 1), :], 0.0)
                w = jnp.sum(q_i * jnp.exp(d) * k_sc[pl.ds(s, 1), :], axis=-1, keepdims=True)
                o_i = o_i + jnp.where(row_s >= s - r0, w, 0.0) * v_sc[pl.ds(s, 1), :]
            o_parts.append(o_i)
        o = jnp.concatenate(o_parts, axis=0)
        o_ref[rows, lanes] = _hgrn_out(o, hg_ref[rows, lanes], nw).astype(o_ref.dtype)
        last = cum[c - 1:c]
        kd = k * jnp.exp(last - cum)
        upd = lax.dot_general(v_b, _bf(kd), (((0,), (0,)), ((), ())), preferred_element_type=jnp.float32)
        st_ref[hh] = st * jnp.exp(last) + upd

    def chunk(ci, carry):
        rows = pl.ds(pl.multiple_of(ci * c, c), c)
        for hh in range(HG_HEADS_PER_STEP):
            head_chunk(hh, rows)
        return carry

    lax.fori_loop(0, t // c, chunk, 0)
    for hh in range(HG_HEADS_PER_STEP):
        s_ref[hh] = st_ref[hh].T


def hgrn_prompt(z, lb, norm_w, b, t):
    tri = jnp.asarray(np.tril(np.ones((HG_KCHUNK, HG_KCHUNK))), jnp.bfloat16)
    hp = HG_HEADS_PER_STEP
    width = hp * HG_DK
    col = lambda off: (lambda i, h: (i, off // width + h))
    vec = pl.BlockSpec((1, width), lambda i, h: (0, h))
    return pl.pallas_call(
        functools.partial(_hgrn_prompt_kernel, t=t),
        grid=(b, HG_H // hp),
        in_specs=[pl.BlockSpec((t, width), col(OFF_HQ)), pl.BlockSpec((t, width), col(OFF_HF)),
                  pl.BlockSpec((t, width), col(OFF_HI)), pl.BlockSpec((t, width), col(OFF_HGATE)),
                  vec, vec, pl.BlockSpec(tri.shape, lambda i, h: (0, 0))],
        out_specs=[pl.BlockSpec((t, width), lambda i, h: (i, h)),
                   pl.BlockSpec((None, hp, HG_DK, HG_DV), lambda i, h: (i, h, 0, 0))],
        out_shape=[jax.ShapeDtypeStruct((b * t, MIX_W), jnp.bfloat16),
                   jax.ShapeDtypeStruct((b, HG_H, HG_DK, HG_DV), jnp.float32)],
        scratch_shapes=[pltpu.VMEM((hp, HG_DV, HG_DK), jnp.float32),
                        pltpu.VMEM((hp, 3, HG_KCHUNK, HG_DK), jnp.float32)],
        compiler_params=_cparams(("parallel", "parallel")),
        name="hgrn_prompt",
    )(z, z, z, z, lb.reshape(1, MIX_W), norm_w.reshape(1, MIX_W), tri)


SEL_ROWS = 40


def _qbd(q, heads_per_kv):
    n_kv = q.shape[0] // heads_per_kv
    row = lax.broadcasted_iota(jnp.int32, q.shape, 0)
    blocks = [jnp.where(row // heads_per_kv == g, q, 0.0) for g in range(n_kv)]
    top = jnp.concatenate(blocks, axis=1)
    return jnp.concatenate([top, jnp.zeros((128 - q.shape[0], top.shape[1]), jnp.float32)], axis=0)


def _softmax_keys(s_sc, s_new, new_ok):
    s = s_sc[...]
    s_new = jnp.where(new_ok, s_new, NEG)
    m = jnp.maximum(jnp.max(s, axis=0, keepdims=True), s_new)
    e = jnp.where(s > 0.5 * NEG, jnp.exp(s - m), 0.0)
    e_new = jnp.where(new_ok, jnp.exp(s_new - m), 0.0)
    l = jnp.sum(e, axis=0, keepdims=True) + e_new
    inv = 1.0 / jnp.where(l > 0.0, l, 1.0)
    return e * inv, e_new * inv


def _weighted_values(p_page, v_of, acc, heads_per_kv):
    rows = p_page.shape[0]
    out = []
    for h in range(len(acc)):
        prod = p_page[:, h:h + 1] * v_of(h // heads_per_kv)
        out.append(acc[h] + jnp.sum(prod.reshape(rows // 8, 8, HEAD_DIM), axis=0))
    return out


def _finish_values(acc, p_new, v_new, heads_per_kv):
    rows = []
    for h in range(len(acc)):
        g = h // heads_per_kv
        rows.append(jnp.sum(acc[h], axis=0, keepdims=True) + p_new[:, h:h + 1] * v_new[:, g * HEAD_DIM:(g + 1) * HEAD_DIM])
    return jnp.concatenate(rows, axis=0)


def _cmp_decode_kernel(pt_ref, *refs, n_pages, pos):
    pages = refs[:n_pages]
    q_ref, w1ab_ref, w2_ref, posrows_ref, ovlt_ref, grp_ref, o_ref, sel_ref, xs = refs[n_pages:]
    n_tok = n_pages * CMP_PER_PAGE
    for pp in range(n_pages // 2):
        for j in range(CMP_STRIDE):
            for c in range(2 * NSA_KVH):
                piece = jnp.concatenate(
                    [_stream(pages[2 * pp + i], c, j, CMP_PER_PAGE, CMP_STRIDE) for i in range(2)], axis=0)
                xs[c, 2 * CMP_PER_PAGE * pp:2 * CMP_PER_PAGE * (pp + 1), j * HEAD_DIM:(j + 1) * HEAD_DIM] = _bf(piece)
    kc = [_compress_tokens(xs[g], w1ab_ref.at[0], w2_ref.at[0], posrows_ref.at[0]) for g in range(NSA_KVH)]
    vc = [_compress_tokens(xs[NSA_KVH + g], w1ab_ref.at[1], w2_ref.at[1], posrows_ref.at[1])
          for g in range(NSA_KVH)]
    qbd = _bf(_qbd(q_ref[...], NSA_REP))
    s = _dot_nt(_bf(jnp.concatenate(kc, axis=1)), qbd) * SCALE
    tok = lax.broadcasted_iota(jnp.int32, s.shape, 0)
    p = _masked_softmax_rows(s, (tok * CMP_STRIDE + (CMP_LEN - 1) <= pos) & (tok < n_tok - 1))
    o_ref[...] = jnp.concatenate(
        [jnp.sum(p[:, h:h + 1] * vc[h // NSA_REP], axis=0, keepdims=True) for h in range(NSA_H)], axis=0)
    imp = _dot_f32_rhs(ovlt_ref[...], _dot_f32_lhs(p, grp_ref[...]))
    blk = lax.broadcasted_iota(jnp.int32, imp.shape, 0)
    cur = pos // SEL_BLOCK
    forced = (blk == 0) | (blk == cur) | (blk == cur - 1)
    score = jnp.where(blk <= cur, imp + jnp.where(forced, FORCE_BONUS, 0.0), NEG)
    n_sel = cur + 1
    rank = _rank_sublanes(score, n_sel)
    sel_ref[...] = jnp.where(rank < float(min(SEL_TOPK, n_sel)), 1.0, 0.0)


def _page_specs(n_pages, layer, heads, phased=False):
    def spec(p):
        if phased:
            return pl.BlockSpec((None, None, PAGE_SIZE, None, heads, HEAD_DIM),
                                lambda b, ph, pt: (layer, pt[b * n_pages + p], 0, ph, 0, 0))
        return pl.BlockSpec((None, None, PAGE_SIZE, 2, heads, HEAD_DIM),
                            lambda b, pt: (layer, pt[b * n_pages + p], 0, 0, 0, 0))
    return [spec(p) for p in range(n_pages)]


def _stream(ref, idx, first=0, count=None, step=1):
    rows = ref.shape[0]
    streams = int(np.prod(ref.shape[1:-1]))
    count = rows if count is None else count
    flat = ref.reshape(rows * streams, ref.shape[-1])
    return flat[pl.ds(first * streams + idx, count, stride=step * streams), :]


def _heads_on_lanes(ref, kind, heads):
    return jnp.concatenate([_stream(ref, kind * heads + h) for h in range(heads)], axis=1)


def nsa_cmp_decode(cache_cmp, layer, pt_flat, q_s, cw, n_pages):
    w1ab, w2, posrows = cw
    b = q_s.shape[0]
    pos = n_pages * PAGE_SIZE
    n_tok = n_pages * CMP_PER_PAGE
    cmp_start = np.arange(n_tok) * CMP_STRIDE
    sel_start = np.arange(SEL_ROWS) * SEL_BLOCK
    ovlt = ((cmp_start[None, :] < sel_start[:, None] + SEL_BLOCK) & (cmp_start[None, :] + CMP_LEN > sel_start[:, None])
            & (np.arange(n_tok)[None, :] < n_tok - 1))
    hh = np.arange(128)
    grp = (hh[:, None] // NSA_REP == hh[None, :] // NSA_REP) & (hh[:, None] < NSA_H) & (hh[None, :] < NSA_H)
    ovlt, grp = jnp.asarray(ovlt, jnp.bfloat16), jnp.asarray(grp, jnp.bfloat16)
    const = lambda a: pl.BlockSpec(a.shape, lambda i, pt: (0,) * a.ndim)
    grid_spec = pltpu.PrefetchScalarGridSpec(
        num_scalar_prefetch=1, grid=(b,),
        in_specs=_page_specs(n_pages, layer, NSA_KVH)
        + [pl.BlockSpec((None, NSA_H, HEAD_DIM), lambda i, pt: (i, 0, 0)),
           const(w1ab), const(w2), const(posrows), const(ovlt), const(grp)],
        out_specs=[pl.BlockSpec((None, NSA_H, HEAD_DIM), lambda i, pt: (i, 0, 0)),
                   pl.BlockSpec((None, SEL_ROWS, 128), lambda i, pt: (i, 0, 0))],
        scratch_shapes=[pltpu.VMEM((2 * NSA_KVH, n_tok, CMP_STRIDE * HEAD_DIM), jnp.bfloat16)])
    return pl.pallas_call(
        functools.partial(_cmp_decode_kernel, n_pages=n_pages, pos=pos),
        grid_spec=grid_spec,
        out_shape=[jax.ShapeDtypeStruct((b, NSA_H, HEAD_DIM), jnp.float32),
                   jax.ShapeDtypeStruct((b, SEL_ROWS, 128), jnp.float32)],
        compiler_params=_cparams(("parallel",)),
        name="nsa_cmp_decode",
    )(pt_flat, *([cache_cmp] * n_pages), q_s, w1ab, w2, posrows, ovlt, grp)


def _slc_decode_kernel(pt_ref, *refs, n_pages, pos):
    pages = refs[:n_pages]
    q_ref, kv_ref, sel_ref, o_ref, s_sc = refs[n_pages:]
    kw = NSA_KVH * HEAD_DIM
    qbd = _bf(_qbd(q_ref[...], NSA_REP))
    row = lax.broadcasted_iota(jnp.int32, (PAGE_SIZE, 128), 0)
    per_page = PAGE_SIZE // SEL_BLOCK
    for p in range(n_pages):
        s = _dot_nt(_bf(_heads_on_lanes(pages[p], 0, NSA_KVH)), qbd) * SCALE
        sel = sel_ref[pl.ds(per_page * p, 1), :]
        for i in range(1, per_page):
            sel = jnp.where(row >= i * SEL_BLOCK, sel_ref[pl.ds(per_page * p + i, 1), :], sel)
        kpos = p * PAGE_SIZE + row
        s_sc[p * PAGE_SIZE:(p + 1) * PAGE_SIZE, :] = jnp.where(kpos <= pos, jnp.where(sel > 0.5, s, NEG), NEG)
    k_new = kv_ref[:, 2 * kw:3 * kw]
    v_new = kv_ref[:, 3 * kw:4 * kw]
    s_new = _dot_nt(_bf(jnp.broadcast_to(k_new, (8, kw))), qbd)[0:1] * SCALE
    new_ok = sel_ref[pl.ds(pos // SEL_BLOCK, 1), :] > 0.5
    _softmax_keys_into(s_sc, s_new, new_ok, o_ref, pages, v_new, NSA_REP, PAGE_SIZE)


def _softmax_keys_into(s_sc, s_new, new_ok, o_ref, pages, v_new, heads_per_kv, rows):
    p_all, p_new = _softmax_keys(s_sc, s_new, new_ok)
    s_sc[...] = p_all
    n_heads = o_ref.shape[0]
    n_kv = n_heads // heads_per_kv
    acc = [jnp.zeros((n_heads, HEAD_DIM), jnp.float32)] * n_kv
    for p, page in enumerate(pages):
        p_bf = _bf(s_sc[p * rows:(p + 1) * rows, :])
        for g in range(n_kv):
            pv = lax.dot_general(p_bf, _bf(_stream(page, n_kv + g)), (((0,), (0,)), ((), ())),
                                 preferred_element_type=jnp.float32)
            acc[g] = acc[g] + pv[0:n_heads]
    head = lax.broadcasted_iota(jnp.int32, (n_heads, 1), 0)
    o = jnp.zeros((n_heads, HEAD_DIM), jnp.float32)
    for g in range(n_kv):
        new = jnp.concatenate([p_new[:, h:h + 1] for h in range(n_heads)], axis=0) * v_new[:, g * HEAD_DIM:(g + 1) * HEAD_DIM]
        o = o + jnp.where(head // heads_per_kv == g, acc[g] + new, 0.0)
    o_ref[...] = o.astype(o_ref.dtype)


def nsa_slc_decode(cache_slc, layer, pt_flat, q_s, kv_new, sel, n_pages):
    b = q_s.shape[0]
    pos = n_pages * PAGE_SIZE
    grid_spec = pltpu.PrefetchScalarGridSpec(
        num_scalar_prefetch=1, grid=(b,),
        in_specs=_page_specs(n_pages, layer, NSA_KVH)
        + [pl.BlockSpec((None, NSA_H, HEAD_DIM), lambda i, pt: (i, 0, 0)),
           pl.BlockSpec((None, 1, N_NKV), lambda i, pt: (i, 0, 0)),
           pl.BlockSpec((None, SEL_ROWS, 128), lambda i, pt: (i, 0, 0))],
        out_specs=pl.BlockSpec((None, NSA_H, HEAD_DIM), lambda i, pt: (i, 0, 0)),
        scratch_shapes=[pltpu.VMEM((n_pages * PAGE_SIZE, 128), jnp.float32)])
    return pl.pallas_call(
        functools.partial(_slc_decode_kernel, n_pages=n_pages, pos=pos),
        grid_spec=grid_spec,
        out_shape=jax.ShapeDtypeStruct((b, NSA_H, HEAD_DIM), jnp.float32),
        compiler_params=_cparams(("parallel",)),
        name="nsa_slc_decode",
    )(pt_flat, *([cache_slc] * n_pages), q_s, kv_new, sel)


def _win_decode_kernel(win_ref, q_ref, kv_ref, ocmp_ref, oslc_ref, gate_ref, o_ref, s_sc, ow_sc, *, pos):
    kw = NSA_KVH * HEAD_DIM
    wb = win_ref.shape[0]
    qbd = _bf(_qbd(q_ref[...], NSA_REP))
    s = _dot_nt(_bf(_heads_on_lanes(win_ref, 0, NSA_KVH)), qbd) * SCALE
    dist = pos - (pos - wb + lax.broadcasted_iota(jnp.int32, s.shape, 0))
    s_sc[...] = jnp.where(dist >= 0, jnp.where(dist <= WINDOW, s, NEG), NEG)
    k_new = kv_ref[:, 4 * kw:5 * kw]
    v_new = kv_ref[:, 5 * kw:6 * kw]
    s_new = _dot_nt(_bf(jnp.broadcast_to(k_new, (8, kw))), qbd)[0:1] * SCALE
    new_ok = jnp.ones(s_new.shape, jnp.float32) > 0.5
    _softmax_keys_into(s_sc, s_new, new_ok, ow_sc, [win_ref], v_new, NSA_REP, wb)
    g = jax.nn.sigmoid(gate_ref[...])
    o_ref[...] = (g[0] * ocmp_ref[...] + g[1] * oslc_ref[...] + g[2] * ow_sc[...]).astype(o_ref.dtype)


def nsa_win_decode(cache_win, layer, q_s, kv_new, o_cmp, o_slc, gate_cols, pos):
    b, wb = cache_win.shape[1], cache_win.shape[2]
    hspec = pl.BlockSpec((None, NSA_H, HEAD_DIM), lambda i: (i, 0, 0))
    return pl.pallas_call(
        functools.partial(_win_decode_kernel, pos=pos),
        grid=(b,),
        in_specs=[pl.BlockSpec((None, None, wb, 2, NSA_KVH, HEAD_DIM), lambda i: (layer, i, 0, 0, 0, 0)), hspec,
                  pl.BlockSpec((None, 1, N_NKV), lambda i: (i, 0, 0)), hspec, hspec,
                  pl.BlockSpec((None, 3, NSA_H, 1), lambda i: (i, 0, 0, 0))],
        out_specs=hspec,
        out_shape=jax.ShapeDtypeStruct((b, NSA_H, HEAD_DIM), jnp.bfloat16),
        scratch_shapes=[pltpu.VMEM((wb, 128), jnp.float32), pltpu.VMEM((NSA_H, HEAD_DIM), jnp.float32)],
        compiler_params=_cparams(("parallel",)),
        name="nsa_win_decode",
    )(cache_win, q_s, kv_new, o_cmp, o_slc, gate_cols)


def _moba_decode_kernel(pt_ref, *refs, n_pages, pos):
    pages = refs[:n_pages]
    q_ref, kn_ref, vn_ref, o_ref, s_sc, pn_sc = refs[n_pages:]
    phase = pl.program_id(1)
    per_blk = MOBA_BLOCK // PAGE_SIZE
    nb = n_pages // per_blk

    @pl.when(phase == 0)
    def _():
        qf = _qbd(q_ref[...], 1)
        qbd = _bf(qf)
        ksum = []
        for p in range(n_pages):
            k = jnp.concatenate([_stream(pages[p], h) for h in range(MOBA_H)], axis=1)
            s_sc[p * PAGE_SIZE:(p + 1) * PAGE_SIZE, :] = _dot_nt(_bf(k), qbd) * SCALE
            ksum.append(jnp.sum(k, axis=0, keepdims=True))
        kmean = jnp.concatenate(
            [sum(ksum[n * per_blk:(n + 1) * per_blk]) * (1.0 / MOBA_BLOCK) for n in range(nb)], axis=0)
        blk = lax.broadcasted_iota(jnp.int32, (nb, 128), 0)
        cur = pos // MOBA_BLOCK
        gate = jnp.where(blk < cur, _dot_nt_f32(kmean, qf), NEG)
        rank = _rank_sublanes(gate, nb)
        sel = jnp.where(rank < float(min(cur, MOBA_TOPK, nb)), 1.0, 0.0)
        row = lax.broadcasted_iota(jnp.int32, (PAGE_SIZE, 128), 0)
        for p in range(n_pages):
            ok = (jnp.broadcast_to(sel[p // per_blk:p // per_blk + 1, :], (PAGE_SIZE, 128)) > 0.5)
            kpos = p * PAGE_SIZE + row
            rows = slice(p * PAGE_SIZE, (p + 1) * PAGE_SIZE)
            s_sc[rows, :] = jnp.where(kpos <= pos, jnp.where(ok, s_sc[rows, :], NEG), NEG)
        s_new = _dot_nt(_bf(jnp.broadcast_to(kn_ref[...], (8, MIX_W))), qbd)[0:1] * SCALE
        new_ok = jnp.ones(s_new.shape, jnp.float32) > 0.5
        p_all, p_new = _softmax_keys(s_sc, s_new, new_ok)
        s_sc[...] = p_all
        pn_sc[...] = jnp.broadcast_to(p_new, pn_sc.shape)

    @pl.when(phase == 1)
    def _():
        acc = [jnp.zeros((8, HEAD_DIM), jnp.float32)] * MOBA_H
        for p in range(n_pages):
            acc = _weighted_values(s_sc[p * PAGE_SIZE:(p + 1) * PAGE_SIZE, :],
                                   lambda h, page=pages[p]: _stream(page, h), acc, 1)
        o_ref[...] = _finish_values(acc, pn_sc[0:1, :], vn_ref[...], 1).astype(o_ref.dtype)


def moba_decode(cache_moba, layer, pt_flat, q_m, k_new, v_new, n_pages):
    b = q_m.shape[0]
    pos = n_pages * PAGE_SIZE
    grid_spec = pltpu.PrefetchScalarGridSpec(
        num_scalar_prefetch=1, grid=(b, 2),
        in_specs=_page_specs(n_pages, layer, MOBA_H, phased=True)
        + [pl.BlockSpec((None, MOBA_H, HEAD_DIM), lambda i, ph, pt: (i, 0, 0)),
           pl.BlockSpec((None, 1, MIX_W), lambda i, ph, pt: (i, 0, 0)),
           pl.BlockSpec((None, 1, MIX_W), lambda i, ph, pt: (i, 0, 0))],
        out_specs=pl.BlockSpec((None, MOBA_H, HEAD_DIM), lambda i, ph, pt: (i, 0, 0)),
        scratch_shapes=[pltpu.VMEM((n_pages * PAGE_SIZE, 128), jnp.float32), pltpu.VMEM((8, 128), jnp.float32)])
    return pl.pallas_call(
        functools.partial(_moba_decode_kernel, n_pages=n_pages, pos=pos),
        grid_spec=grid_spec,
        out_shape=jax.ShapeDtypeStruct((b, MOBA_H, HEAD_DIM), jnp.bfloat16),
        compiler_params=_cparams(("parallel", "arbitrary")),
        name="moba_decode",
    )(pt_flat, *([cache_moba] * n_pages), q_m, k_new, v_new)


def _hgrn_decode_kernel(s_ref, hq_ref, hf_ref, hi_ref, hg_ref, lb_ref, nw_ref, o_ref, so_ref):
    for h in range(HG_H):
        lanes = slice(h * HG_DV, (h + 1) * HG_DV)
        q, k, logf = _hgrn_gates(hq_ref[h], hf_ref[h], lb_ref[h])
        s_new = jnp.exp(logf) * s_ref[h] + k * hi_ref[:, lanes]
        so_ref[h] = s_new
        o = jnp.sum(q * s_new, axis=0, keepdims=True)
        o_ref[:, lanes] = _hgrn_out(o, hg_ref[:, lanes], nw_ref[:, lanes]).astype(o_ref.dtype)


def hgrn_decode(state, layer, hq_col, hf_col, hi_row, hg_row, lb, norm_w):
    b = state.shape[1]
    colspec = pl.BlockSpec((None, HG_H, HG_DK, 1), lambda i: (i, 0, 0, 0))
    rowspec = pl.BlockSpec((None, 1, MIX_W), lambda i: (i, 0, 0))
    return pl.pallas_call(
        _hgrn_decode_kernel,
        grid=(b,),
        in_specs=[pl.BlockSpec((None, None, HG_H, HG_DK, HG_DV), lambda i: (layer, i, 0, 0, 0)),
                  colspec, colspec, rowspec, rowspec,
                  pl.BlockSpec((HG_H, HG_DK, 1), lambda i: (0, 0, 0)),
                  pl.BlockSpec((1, MIX_W), lambda i: (0, 0))],
        out_specs=[rowspec, pl.BlockSpec((None, HG_H, HG_DK, HG_DV), lambda i: (i, 0, 0, 0))],
        out_shape=[jax.ShapeDtypeStruct((b, 1, MIX_W), jnp.bfloat16),
                   jax.ShapeDtypeStruct((b, HG_H, HG_DK, HG_DV), jnp.float32)],
        compiler_params=_cparams(("parallel",)),
        name="hgrn_decode",
    )(state, hq_col, hf_col, hi_row, hg_row, lb.reshape(HG_H, HG_DK, 1), norm_w.reshape(1, MIX_W))


def masked_probs(s, mask):
    p = jax.nn.softmax(jnp.where(mask, s, NEG), axis=-1)
    return jnp.where(mask, p, 0.0)


def map_query_chunks(fn, *xs):
    n_q = xs[0].shape[0]
    qc = math.gcd(n_q, Q_CHUNK)
    n = n_q // qc
    blocks = tuple(a.reshape((n, qc) + a.shape[1:]) for a in xs)
    out = lax.map(lambda a: fn(*a), blocks)
    return out.reshape((n_q,) + out.shape[2:])


def gather_pages(pool, pages):
    rows = pool[pages]
    return rows.reshape((-1,) + pool.shape[2:])


def nsa_compress(x, pos_emb, w1, w2):
    n_blk = (x.shape[0] - CMP_LEN) // CMP_STRIDE + 1
    idx = np.arange(n_blk)[:, None] * CMP_STRIDE + np.arange(CMP_LEN)[None, :]
    blk = x[idx] + pos_emb[None, :, None, :]
    blk = blk.transpose(0, 2, 1, 3).reshape(n_blk, x.shape[1], CMP_LEN * HEAD_DIM)
    return jax.nn.silu(blk @ w1) @ w2


def nsa_seq(q, q_pos, kc, vc, ks, vs, cmp_pos, cmp_w1, cmp_w2):
    n_q, seq_len = q.shape[0], kc.shape[0]
    rep = NSA_H // NSA_KVH
    scale = HEAD_DIM ** -0.5
    qg = q.reshape(n_q, NSA_KVH, rep, HEAD_DIM)
    k_cmp = nsa_compress(kc, cmp_pos[0], cmp_w1[0], cmp_w2[0])
    v_cmp = nsa_compress(vc, cmp_pos[1], cmp_w1[1], cmp_w2[1])
    n_cmp = k_cmp.shape[0]
    cmp_start = np.arange(n_cmp, dtype=np.int32) * CMP_STRIDE
    s = jnp.einsum('qgrd,ngd->qgrn', qg, k_cmp, preferred_element_type=jnp.float32) * scale
    cmask = (cmp_start + CMP_LEN - 1)[None, :] <= q_pos[:, None]
    p_cmp = masked_probs(s, cmask[:, None, None, :])
    o_cmp = jnp.einsum('qgrn,ngd->qgrd', p_cmp.astype(v_cmp.dtype), v_cmp)
    n_sel = -(-seq_len // SEL_BLOCK)
    sel_start = np.arange(n_sel, dtype=np.int32) * SEL_BLOCK
    overlap = ((cmp_start[:, None] < sel_start[None, :] + SEL_BLOCK)
               & (cmp_start[:, None] + CMP_LEN > sel_start[None, :])).astype(np.float32)
    imp = jnp.einsum('qgrn,nj->qgj', p_cmp, overlap, precision=lax.Precision.HIGHEST)
    cur = q_pos // SEL_BLOCK
    blk = np.arange(n_sel, dtype=np.int32)[None, :]
    valid = blk <= cur[:, None]
    forced = (blk == 0) | (blk == cur[:, None]) | (blk == cur[:, None] - 1)
    score = jnp.where(valid[:, None, :],
                      imp + jnp.where(forced, FORCE_BONUS, 0.0)[:, None, :], NEG)
    n_top = min(SEL_TOPK, n_sel)
    _, idx = lax.top_k(score, n_top)
    ok = np.arange(n_top, dtype=np.int32)[None, :] < jnp.minimum(cur + 1, n_top)[:, None]
    pad = n_sel * SEL_BLOCK - seq_len
    ksb = jnp.pad(ks, ((0, pad), (0, 0), (0, 0))).reshape(
        n_sel, SEL_BLOCK, NSA_KVH, HEAD_DIM).transpose(2, 0, 1, 3)
    vsb = jnp.pad(vs, ((0, pad), (0, 0), (0, 0))).reshape(
        n_sel, SEL_BLOCK, NSA_KVH, HEAD_DIM).transpose(2, 0, 1, 3)
    g_ix = np.arange(NSA_KVH)[None, :, None]
    offs = np.arange(SEL_BLOCK, dtype=np.int32)

    def sel_chunk(q_c, pos_c, idx_c, ok_c):
        n = q_c.shape[0]
        k_g = ksb[g_ix, idx_c]
        v_g = vsb[g_ix, idx_c]
        k_pos = idx_c[..., None] * SEL_BLOCK + offs
        m = (k_pos <= pos_c[:, None, None, None]) & ok_c[:, None, :, None]
        s_c = jnp.einsum('qgrd,qgkbd->qgrkb', q_c, k_g, preferred_element_type=jnp.float32) * scale
        p_c = masked_probs(s_c.reshape(n, NSA_KVH, rep, -1), m.reshape(n, NSA_KVH, 1, -1))
        return jnp.einsum('qgrkb,qgkbd->qgrd', p_c.reshape(s_c.shape).astype(v_g.dtype), v_g)

    o_slc = map_query_chunks(sel_chunk, qg, q_pos, idx, ok)
    return o_cmp.reshape(n_q, NSA_H, HEAD_DIM), o_slc.reshape(n_q, NSA_H, HEAD_DIM)


def window_banded(q, k, v):
    b, t = q.shape[:2]
    rep = NSA_H // NSA_KVH
    qb = math.gcd(t, WIN_QBLOCK)
    nb = t // qb
    span = WINDOW + qb
    idx = np.arange(nb)[:, None] * qb + np.arange(span)[None, :]
    k_pos = idx - WINDOW
    q_pos = np.arange(t).reshape(nb, qb)
    d = q_pos[:, :, None] - k_pos[:, None, :]
    mask = (d >= 0) & (d <= WINDOW) & (k_pos[:, None, :] >= 0)
    kp = jnp.pad(k, ((0, 0), (WINDOW, 0), (0, 0), (0, 0)))[:, idx]
    vp = jnp.pad(v, ((0, 0), (WINDOW, 0), (0, 0), (0, 0)))[:, idx]
    qg = q.reshape(b, nb, qb, NSA_KVH, rep, HEAD_DIM)
    s = jnp.einsum('bnqgrd,bnkgd->bnqgrk', qg, kp, preferred_element_type=jnp.float32) * (HEAD_DIM ** -0.5)
    p = masked_probs(s, mask[None, :, :, None, None, :])
    o = jnp.einsum('bnqgrk,bnkgd->bnqgrd', p.astype(vp.dtype), vp)
    return o.reshape(b, t, NSA_H, HEAD_DIM)


def window_direct(q, q_pos, k, v, k_pos):
    b, n_q = q.shape[:2]
    rep = NSA_H // NSA_KVH
    qg = q.reshape(b, n_q, NSA_KVH, rep, HEAD_DIM)
    s = jnp.einsum('bqgrd,bkgd->bqgrk', qg, k, preferred_element_type=jnp.float32) * (HEAD_DIM ** -0.5)
    d = q_pos[:, None] - k_pos[None, :]
    mask = (d >= 0) & (d <= WINDOW)
    p = masked_probs(s, mask[None, :, None, None, :])
    o = jnp.einsum('bqgrk,bkgd->bqgrd', p.astype(v.dtype), v)
    return o.reshape(b, n_q, NSA_H, HEAD_DIM)


def nsa_combine(gate_logits, o_cmp, o_slc, o_win):
    b, t = o_cmp.shape[:2]
    g = jax.nn.sigmoid(gate_logits.astype(jnp.float32)).reshape(b, t, 3, NSA_H, 1)
    o = g[:, :, 0] * o_cmp + g[:, :, 1] * o_slc + g[:, :, 2] * o_win
    return o.reshape(b, t, MIX_W).astype(o_cmp.dtype)


def moba_seq(q, q_pos, k, v):
    n_q, seq_len = q.shape[0], k.shape[0]
    scale = HEAD_DIM ** -0.5
    nb = -(-seq_len // MOBA_BLOCK)
    pad = nb * MOBA_BLOCK - seq_len
    kb = jnp.pad(k, ((0, pad), (0, 0), (0, 0))).reshape(nb, MOBA_BLOCK, MOBA_H, HEAD_DIM)
    vb = jnp.pad(v, ((0, pad), (0, 0), (0, 0))).reshape(nb, MOBA_BLOCK, MOBA_H, HEAD_DIM)
    k_mean = jnp.mean(kb.astype(jnp.float32), axis=1)
    cur = q_pos // MOBA_BLOCK
    gate = jnp.einsum('qhd,nhd->qhn', q.astype(jnp.float32), k_mean, precision=lax.Precision.HIGHEST)
    past = np.arange(nb, dtype=np.int32)[None, :] < cur[:, None]
    gate = jnp.where(past[:, None, :], gate, NEG)
    n_top = min(MOBA_TOPK, nb)
    _, idx = lax.top_k(gate, n_top)
    ok = np.arange(n_top, dtype=np.int32)[None, :] < jnp.minimum(cur, n_top)[:, None]
    idx = jnp.concatenate([idx, jnp.broadcast_to(cur[:, None, None], (n_q, MOBA_H, 1))], axis=-1)
    ok = jnp.concatenate([ok, jnp.ones((n_q, 1), dtype=bool)], axis=-1)
    kbh = kb.transpose(2, 0, 1, 3)
    vbh = vb.transpose(2, 0, 1, 3)
    h_ix = np.arange(MOBA_H)[None, :, None]
    offs = np.arange(MOBA_BLOCK, dtype=np.int32)

    def chunk(q_c, pos_c, idx_c, ok_c):
        n = q_c.shape[0]
        k_g = kbh[h_ix, idx_c]
        v_g = vbh[h_ix, idx_c]
        k_pos = idx_c[..., None] * MOBA_BLOCK + offs
        m = (k_pos <= pos_c[:, None, None, None]) & ok_c[:, None, :, None]
        s_c = jnp.einsum('qhd,qhkbd->qhkb', q_c, k_g, preferred_element_type=jnp.float32) * scale
        p_c = masked_probs(s_c.reshape(n, MOBA_H, -1), m.reshape(n, MOBA_H, -1))
        return jnp.einsum('qhkb,qhkbd->qhd', p_c.reshape(s_c.shape).astype(v_g.dtype), v_g)

    return map_query_chunks(chunk, q, q_pos, idx, ok)


def hgrn_features(hq, hf, hi, lb):
    b, t, _ = hq.shape
    q = jax.nn.silu(hq.astype(jnp.float32)).reshape(b, t, HG_H, HG_DK)
    z = hf.astype(jnp.float32).reshape(b, t, HG_H, HG_DK)
    lbh = lb.astype(jnp.float32).reshape(HG_H, HG_DK)
    f = lbh + (1.0 - lbh) * jax.nn.sigmoid(z)
    logf = jnp.log(jnp.maximum(f, F_FLOOR))
    k = (1.0 - lbh) * jax.nn.sigmoid(-z)
    v = hi.astype(jnp.float32).reshape(b, t, HG_H, HG_DV)
    return q, k, logf, v


def hgrn_chunked(q, k, logf, v, s0):
    b, t, h, _ = q.shape
    c = math.gcd(t, HG_CHUNK)
    n = t // c
    tri = np.tril(np.ones((c, c), dtype=bool))[:, :, None]

    def blocks(a):
        return a.reshape(b, n, c, h, a.shape[-1]).transpose(1, 0, 3, 2, 4)

    def step(s, inp):
        q_c, k_c, l_c, v_c = inp
        cum = jnp.cumsum(l_c, axis=2)
        diff = cum[:, :, :, None, :] - cum[:, :, None, :, :]
        decay = jnp.where(tri, jnp.exp(jnp.where(tri, diff, 0.0)), 0.0)
        a = jnp.einsum('bhtd,bhsd,bhtsd->bhts', q_c, k_c, decay)
        o = jnp.einsum('bhts,bhse->bhte', a, v_c) + jnp.einsum('bhtd,bhde->bhte', q_c * jnp.exp(cum), s)
        last = cum[:, :, -1]
        s = jnp.exp(last)[..., None] * s + jnp.einsum(
            'bhsd,bhse->bhde', k_c * jnp.exp(last[:, :, None] - cum), v_c)
        return s, o

    s, o = lax.scan(step, s0, (blocks(q), blocks(k), blocks(logf), blocks(v)))
    return o.transpose(1, 0, 3, 2, 4).reshape(b, t, h, v.shape[-1]), s


def hgrn_recurrent(q, k, logf, v, s0):
    def step(s, inp):
        q_t, k_t, l_t, v_t = inp
        s = jnp.exp(l_t)[..., None] * s + k_t[..., None] * v_t[..., None, :]
        return s, jnp.einsum('bhd,bhde->bhe', q_t, s)

    sw = lambda a: jnp.swapaxes(a, 0, 1)
    s, o = lax.scan(step, s0, (sw(q), sw(k), sw(logf), sw(v)))
    return sw(o), s


def hgrn_output(o, hg, norm_w):
    b, t = o.shape[:2]
    o = o * lax.rsqrt(jnp.mean(o * o, axis=-1, keepdims=True) + EPS)
    o = o * norm_w.astype(jnp.float32).reshape(HG_H, HG_DV)
    g = jax.nn.silu(hg.astype(jnp.float32)).reshape(b, t, HG_H, HG_DV)
    return (o * g).reshape(b, t, MIX_W).astype(hg.dtype)


def split_z(z, b, t):
    z = z.reshape(b, t, D_IN_PAD)
    sl = lambda off, n: z[:, :, off:off + n]
    return (sl(OFF_NQ, N_NQ), sl(OFF_NKV, N_NKV), sl(OFF_NGATE, N_NGATE), sl(OFF_MQKV, N_MQKV),
            sl(OFF_HQ, N_HG), sl(OFF_HF, N_HG), sl(OFF_HI, N_HG), sl(OFF_HGATE, N_HG))


def finish_layer(x2, z, o_nsa, o_hg, o_moba, lw):
    mixed = branch_merge((o_nsa, o_hg, o_moba), lw["w_branch"], z)
    x2 = matmul(mixed, lw["w_out"], epilogue="residual", residual=x2)
    h2 = rmsnorm_rows(x2, lw["norm2_w"], jnp.bfloat16)
    u2 = matmul(h2, lw["w_up"], epilogue="relu2", out_dtype=jnp.bfloat16)
    return matmul(u2, lw["w_down"], epilogue="residual", residual=x2)


def prompt_layer(x, lb, lw):
    b, t, _ = x.shape
    x2 = x.reshape(b * t, D_MODEL)
    h = rmsnorm_rows(x2, lw["norm1_w"], jnp.bfloat16)
    z = matmul(h, lw["w_in"])
    ckv = nsa_compress_prompt(z, b, t, lw["cw"])
    o_nsa = nsa_prompt(z, ckv, b, t)
    o_moba = moba_prompt(z, b, t)
    o_hg, s_fin = hgrn_prompt(z, lb, lw["hg_norm_w"], b, t)
    x2 = finish_layer(x2, z, o_nsa, o_hg, o_moba, lw)
    return (x2.reshape(b, t, D_MODEL),) + kv_cache_rows(z, b, t) + (s_fin,)


def kv_cache_rows(z, b, t):
    z3 = z.reshape(b, t, D_IN_PAD)
    gw = 2 * NSA_KVH * HEAD_DIM
    kv = lambda kind: z3[:, :, OFF_NKV + kind * gw:OFF_NKV + (kind + 1) * gw].reshape(b, t, 2, NSA_KVH, HEAD_DIM)
    mkv = z3[:, :, OFF_MQKV + MIX_W:OFF_MQKV + 3 * MIX_W].reshape(b, t, 2, MOBA_H, HEAD_DIM)
    return kv(0), kv(1), mkv, kv(2)[:, max(t - WINDOW, 0):]


def sample_layer(x, layer, caches, page_table, lb, lw):
    b, t, _ = x.shape
    assert t == 1, "the decode kernels take one new token per sequence"
    x2 = x.reshape(b * t, D_MODEL)
    h = rmsnorm_rows(x2, lw["norm1_w"], jnp.bfloat16)
    z = matmul(h, lw["w_in"])
    o_nsa, o_hg, o_moba, s_new = sample_mixers(z, layer, caches, page_table, lb, lw)
    x2 = finish_layer(x2, z, o_nsa, o_hg, o_moba, lw)
    c_cmp, c_slc, c_moba, win_new = kv_cache_rows(z, b, t)
    win_out = jnp.concatenate([caches[3][layer][:, t:], win_new], axis=1)
    return (x2.reshape(b, t, D_MODEL), c_cmp, c_slc, c_moba, win_out, s_new)


def sample_mixers(z, layer, caches, page_table, lb, lw):
    cache_cmp, cache_slc, cache_moba, cache_win, state = caches
    b = z.shape[0]
    n_pages = page_table.shape[1]
    pos = n_pages * PAGE_SIZE
    pt_flat = page_table.reshape(-1)
    col = lambda off, n: z[:, off:off + n]
    q_s = col(OFF_NQ, N_NQ).reshape(b, NSA_H, HEAD_DIM)
    kv_new = col(OFF_NKV, N_NKV).reshape(b, 1, N_NKV)
    gate_cols = col(OFF_NGATE, N_NGATE).reshape(b, 3, NSA_H, 1)
    o_cmp, sel = nsa_cmp_decode(cache_cmp, layer, pt_flat, q_s, lw["cw"], n_pages)
    o_slc = nsa_slc_decode(cache_slc, layer, pt_flat, q_s, kv_new, sel, n_pages)
    o_nsa = nsa_win_decode(cache_win, layer, q_s, kv_new, o_cmp, o_slc, gate_cols, pos)
    q_m = col(OFF_MQKV, MIX_W).reshape(b, MOBA_H, HEAD_DIM)
    k_new = col(OFF_MQKV + MIX_W, MIX_W).reshape(b, 1, MIX_W)
    v_new = col(OFF_MQKV + 2 * MIX_W, MIX_W).reshape(b, 1, MIX_W)
    o_moba = moba_decode(cache_moba, layer, pt_flat, q_m, k_new, v_new, n_pages)
    o_hg, s_new = hgrn_decode(state, layer, col(OFF_HQ, N_HG).reshape(b, HG_H, HG_DK, 1),
                              col(OFF_HF, N_HG).reshape(b, HG_H, HG_DK, 1), col(OFF_HI, N_HG).reshape(b, 1, MIX_W),
                              col(OFF_HGATE, N_HG).reshape(b, 1, MIX_W), lb, lw["hg_norm_w"])
    return o_nsa.reshape(b, MIX_W), o_hg.reshape(b, MIX_W), o_moba.reshape(b, MIX_W), s_new


def prep_layer_weights(i, norm1_w, norm2_w, w_in, nsa_cmp_pos, nsa_cmp_w1, nsa_cmp_w2, hgrn_norm_w,
                       w_branch, w_out, w_up, w_down):
    wi = w_in[i]
    gate_lo = N_NQ + N_NKV
    gate_hi = gate_lo + N_NGATE
    pad = jnp.zeros((D_MODEL, D_IN_PAD - wi.shape[1]), wi.dtype)
    wi = jnp.concatenate([wi[:, :gate_lo], wi[:, gate_hi:], wi[:, gate_lo:gate_hi], pad], axis=1)
    bf = lambda a: a.astype(jnp.bfloat16)
    return dict(norm1_w=norm1_w[i], norm2_w=norm2_w[i], w_in=bf(wi),
                cw=prep_compress_weights(nsa_cmp_pos[i], nsa_cmp_w1[i], nsa_cmp_w2[i]), hg_norm_w=hgrn_norm_w[i],
                w_branch=bf(w_branch[i]), w_out=bf(w_out[i]), w_up=bf(w_up[i]), w_down=bf(w_down[i]))


def kernel(x_prompt, x_sample, cache_nsa_cmp, cache_nsa_slc, cache_moba, cache_nsa_win, state_hgrn,
           page_table, norm1_w, norm2_w, w_in, nsa_cmp_pos, nsa_cmp_w1, nsa_cmp_w2, hgrn_lb_logits,
           hgrn_norm_w, w_branch, w_out, w_up, w_down, final_norm_w):
    sm = jax.nn.softmax(hgrn_lb_logits.astype(jnp.float32), axis=0)
    lbs = jnp.cumsum(sm, axis=0) - sm[0:1]
    xp, xs = x_prompt, x_sample
    outs_p, outs_s = [], []
    for i in range(DEPTH):
        lw = prep_layer_weights(i, norm1_w, norm2_w, w_in, nsa_cmp_pos, nsa_cmp_w1, nsa_cmp_w2,
                                hgrn_norm_w, w_branch, w_out, w_up, w_down)
        xp, *st_p = prompt_layer(xp, lbs[i], lw)
        xs, *st_s = sample_layer(xs, i, (cache_nsa_cmp, cache_nsa_slc, cache_moba, cache_nsa_win, state_hgrn),
                                 page_table, lbs[i], lw)
        outs_p.append(st_p)
        outs_s.append(st_s)
    y_prompt = rmsnorm_rows(xp.reshape(-1, D_MODEL), final_norm_w, jnp.float32).reshape(xp.shape)
    y_sample = rmsnorm_rows(xs.reshape(-1, D_MODEL), final_norm_w, jnp.float32).reshape(xs.shape)
    stack = lambda outs, j: jnp.stack([o[j] for o in outs])
    return (y_prompt, y_sample) + tuple(stack(outs_p, j) for j in range(5)) + tuple(
        stack(outs_s, j) for j in range(5))
```

```python
import functools
import math

import jax
import jax.numpy as jnp
import numpy as np
from jax import lax
from jax.experimental import pallas as pl
from jax.experimental.pallas import tpu as pltpu

D_MODEL = 2048
DEPTH = 2
PAGE_SIZE = 128
HEAD_DIM = 128
MIX_W = D_MODEL // 2
N_BRANCH = 3
NSA_H = MIX_W // HEAD_DIM
NSA_KVH = NSA_H // 4
CMP_LEN = 32
CMP_STRIDE = 16
SEL_BLOCK = 64
SEL_TOPK = 16
WINDOW = 512
WIN_QBLOCK = 128
FORCE_BONUS = 1e4
MOBA_H = MIX_W // HEAD_DIM
MOBA_BLOCK = 256
MOBA_TOPK = 3
HG_H = MIX_W // HEAD_DIM
HG_DK = 128
HG_DV = MIX_W // HG_H
HG_CHUNK = 64
D_FF = 4 * D_MODEL
Q_CHUNK = 32
EPS = 1e-6
F_FLOOR = 1e-30
NEG = -1e30

N_NQ = NSA_H * HEAD_DIM
N_NKV = 6 * NSA_KVH * HEAD_DIM
N_NGATE = 3 * NSA_H
N_MQKV = 3 * MOBA_H * HEAD_DIM
N_HG = HG_H * HG_DK
N_MG = N_BRANCH * D_MODEL
OFF_NQ = 0
OFF_NKV = OFF_NQ + N_NQ
OFF_MQKV = OFF_NKV + N_NKV
OFF_HQ = OFF_MQKV + N_MQKV
OFF_HF = OFF_HQ + N_HG
OFF_HI = OFF_HF + N_HG
OFF_HGATE = OFF_HI + N_HG
OFF_MG = OFF_HGATE + N_HG
OFF_NGATE = OFF_MG + N_MG
D_IN_PAD = 16384

VMEM_LIMIT_BYTES = 48 * 1024 * 1024


def _cparams(sem):
    return pltpu.CompilerParams(dimension_semantics=sem, vmem_limit_bytes=VMEM_LIMIT_BYTES)


def _rmsnorm_kernel(x_ref, w_ref, o_ref):
    x = x_ref[...]
    y = x * lax.rsqrt(jnp.mean(x * x, axis=-1, keepdims=True) + EPS)
    o_ref[...] = (y * w_ref[...]).astype(o_ref.dtype)


def rmsnorm_rows(x, w, out_dtype):
    m, d = x.shape
    tm = min(m, 512)
    return pl.pallas_call(
        _rmsnorm_kernel,
        grid=(m // tm,),
        in_specs=[pl.BlockSpec((tm, d), lambda i: (i, 0)), pl.BlockSpec((1, d), lambda i: (0, 0))],
        out_specs=pl.BlockSpec((tm, d), lambda i: (i, 0)),
        out_shape=jax.ShapeDtypeStruct((m, d), out_dtype),
        compiler_params=_cparams(("parallel",)),
        name="rmsnorm",
    )(x, w.reshape(1, d))


def _mm_kernel(*refs, epilogue, nk):
    if epilogue == "residual":
        a_ref, w_ref, r_ref, o_ref = refs[:4]
        rest = refs[4:]
    else:
        a_ref, w_ref, o_ref = refs[:3]
        r_ref = None
        rest = refs[3:]
    part = jnp.dot(a_ref[...], w_ref[...].astype(jnp.bfloat16), preferred_element_type=jnp.float32)

    def finish(acc):
        if epilogue == "residual":
            o_ref[...] = r_ref[...] + acc
        elif epilogue == "relu2":
            u = jnp.maximum(acc, 0.0)
            o_ref[...] = (u * u).astype(o_ref.dtype)
        else:
            o_ref[...] = acc.astype(o_ref.dtype)

    if nk == 1:
        finish(part)
    else:
        acc_ref = rest[0]
        k = pl.program_id(2)

        @pl.when(k == 0)
        def _():
            acc_ref[...] = part

        @pl.when(k > 0)
        def _():
            acc_ref[...] += part

        @pl.when(k == nk - 1)
        def _():
            finish(acc_ref[...])


def matmul(a, w, layer, *, epilogue="none", residual=None, out_dtype=jnp.float32, tm=1024, tn=1024, tk=2048):
    m, kdim = a.shape
    n = w.shape[2]
    tm, tn, tk = min(tm, m), min(tn, n), min(tk, kdim)
    nk = kdim // tk
    in_specs = [pl.BlockSpec((tm, tk), lambda j, i, k: (i, k)),
                pl.BlockSpec((None, tk, tn), lambda j, i, k: (layer, k, j))]
    args = [a, w]
    if epilogue == "residual":
        in_specs.append(pl.BlockSpec((tm, tn), lambda j, i, k: (i, j)))
        args.append(residual)
    scratch = [pltpu.VMEM((tm, tn), jnp.float32)] if nk > 1 else []
    return pl.pallas_call(
        functools.partial(_mm_kernel, epilogue=epilogue, nk=nk),
        grid=(n // tn, m // tm, nk),
        in_specs=in_specs,
        out_specs=pl.BlockSpec((tm, tn), lambda j, i, k: (i, j)),
        out_shape=jax.ShapeDtypeStruct((m, n), out_dtype),
        scratch_shapes=scratch,
        compiler_params=_cparams(("parallel", "parallel", "arbitrary")),
        name="matmul_" + epilogue,
    )(*args)


W_IN_TN = 512


def _w_in_row(j):
    n_lo = (N_NQ + N_NKV) // W_IN_TN
    gate_tile = OFF_NGATE // W_IN_TN
    row = jnp.where(j < n_lo, j * W_IN_TN,
                    jnp.where(j < gate_tile, N_NQ + N_NKV + N_NGATE + (j - n_lo) * W_IN_TN, N_NQ + N_NKV))
    return pl.multiple_of(row, 8)


def _project_kernel(a_ref, wt_ref, o_ref):
    o_ref[...] = lax.dot_general(a_ref[...], wt_ref[0].astype(jnp.bfloat16), (((1,), (1,)), ((), ())),
                                 preferred_element_type=jnp.float32)


def project_in(a, w_in_t, layer, *, tm=1024):
    m, d = a.shape
    assert (N_NQ + N_NKV) % W_IN_TN == 0 and OFF_NGATE % W_IN_TN == 0 and D_IN_PAD == OFF_NGATE + W_IN_TN
    tm = min(tm, m)
    return pl.pallas_call(
        _project_kernel,
        grid=(D_IN_PAD // W_IN_TN, m // tm),
        in_specs=[pl.BlockSpec((tm, d), lambda j, i: (i, 0)),
                  pl.BlockSpec((pl.Element(1), pl.Element(W_IN_TN), pl.Element(d)),
                               lambda j, i: (layer, _w_in_row(j), 0))],
        out_specs=pl.BlockSpec((tm, W_IN_TN), lambda j, i: (i, j)),
        out_shape=jax.ShapeDtypeStruct((m, D_IN_PAD), jnp.float32),
        compiler_params=_cparams(("parallel", "parallel")),
        name="project_in",
    )(a, w_in_t)


def _merge_kernel(b0_ref, b1_ref, b2_ref, wb_ref, g0_ref, g1_ref, g2_ref, o_ref):
    acc = None
    for n, (b_ref, g_ref) in enumerate(((b0_ref, g0_ref), (b1_ref, g1_ref), (b2_ref, g2_ref))):
        proj = jnp.dot(b_ref[...], wb_ref[n].astype(jnp.bfloat16), preferred_element_type=jnp.float32)
        term = jax.nn.sigmoid(g_ref[...]) * proj
        acc = term if acc is None else acc + term
    o_ref[...] = acc.astype(o_ref.dtype)


def branch_merge(branches, wb, layer, z, *, tm=512, tn=512):
    m = branches[0].shape[0]
    tm = min(tm, m)
    gate_specs = [
        pl.BlockSpec((tm, tn), lambda j, i, n=n: (i, (OFF_MG + n * D_MODEL) // tn + j)) for n in range(N_BRANCH)]
    return pl.pallas_call(
        _merge_kernel,
        grid=(D_MODEL // tn, m // tm),
        in_specs=[pl.BlockSpec((tm, MIX_W), lambda j, i: (i, 0))] * N_BRANCH
        + [pl.BlockSpec((None, N_BRANCH, MIX_W, tn), lambda j, i: (layer, 0, 0, j))] + gate_specs,
        out_specs=pl.BlockSpec((tm, tn), lambda j, i: (i, j)),
        out_shape=jax.ShapeDtypeStruct((m, D_MODEL), jnp.bfloat16),
        compiler_params=_cparams(("parallel", "parallel")),
        name="branch_merge",
    )(*branches, wb, z, z, z)


SCALE = HEAD_DIM ** -0.5
NSA_REP = NSA_H // NSA_KVH
CMP_PER_PAGE = PAGE_SIZE // CMP_STRIDE
CMP_HALF = CMP_LEN // CMP_STRIDE * 0 + CMP_STRIDE * HEAD_DIM


def _bf(x):
    return x.astype(jnp.bfloat16)


def _split3(x):
    hi = _bf(x)
    r1 = x - hi.astype(jnp.float32)
    mid = _bf(r1)
    lo = _bf(r1 - mid.astype(jnp.float32))
    return hi, mid, lo


_NT = (((1,), (1,)), ((), ()))


def _dot(a, b):
    return jnp.dot(a, b, preferred_element_type=jnp.float32)


def _dot_nt(a, b):
    return lax.dot_general(a, b, _NT, preferred_element_type=jnp.float32)


def _dot_f32_lhs(a, b_exact):
    return sum(_dot(p, b_exact) for p in _split3(a))


def _dot_f32_rhs(a_exact, b):
    return sum(_dot(a_exact, p) for p in _split3(b))


def _dot_nt_f32(a, b):
    a1, a2, a3 = _split3(a)
    b1, b2, b3 = _split3(b)
    return (_dot_nt(a1, b1) + (_dot_nt(a1, b2) + _dot_nt(a2, b1))
            + (_dot_nt(a1, b3) + _dot_nt(a2, b2) + _dot_nt(a3, b1)))


def _masked_softmax(s, mask):
    s = jnp.where(mask, s, NEG)
    m = jnp.max(s, axis=-1, keepdims=True)
    e = jnp.where(mask, jnp.exp(s - m), 0.0)
    l = jnp.sum(e, axis=-1, keepdims=True)
    return e / jnp.where(l > 0.0, l, 1.0)


def _masked_softmax_rows(s, mask):
    s = jnp.where(mask, s, NEG)
    m = jnp.max(s, axis=0, keepdims=True)
    e = jnp.where(mask, jnp.exp(s - m), 0.0)
    l = jnp.sum(e, axis=0, keepdims=True)
    return e / jnp.where(l > 0.0, l, 1.0)


def _rank_lanes(score, n):
    lane = lax.broadcasted_iota(jnp.int32, score.shape, 1)
    rank = jnp.zeros(score.shape, jnp.float32)
    for i in range(n):
        col = score[:, i:i + 1]
        ahead = jnp.where(col == score, jnp.where(lane > i, 1.0, 0.0), jnp.where(col > score, 1.0, 0.0))
        rank = rank + ahead
    return rank


def _rank_sublanes(score, n):
    row = lax.broadcasted_iota(jnp.int32, score.shape, 0)
    rank = jnp.zeros(score.shape, jnp.float32)
    for i in range(n):
        r = score[i:i + 1, :]
        ahead = jnp.where(r == score, jnp.where(row > i, 1.0, 0.0), jnp.where(r > score, 1.0, 0.0))
        rank = rank + ahead
    return rank


def _compress_tokens(x_bf, w1ab_ref, w2_ref, posrows_ref):
    n = x_bf.shape[0]
    w1ab = w1ab_ref[...]
    pre = _dot(x_bf, w1ab)
    pb = _dot(posrows_ref[...], w1ab)
    posbias = pb[0:1, :HEAD_DIM] + pb[1:2, HEAD_DIM:]
    nxt = pltpu.roll(pre[:, HEAD_DIM:], n - 1, 0)
    hid = pre[:, :HEAD_DIM] + nxt + posbias
    return _dot(_bf(jax.nn.silu(hid)), w2_ref[...])


def prep_compress_weights(cmp_pos, cmp_w1, cmp_w2):
    half = CMP_STRIDE * HEAD_DIM
    w1ab = jnp.concatenate([cmp_w1[:, :half], cmp_w1[:, half:]], axis=-1)
    pos2 = cmp_pos.reshape(2, 2, half)
    posrows = jnp.concatenate([pos2, jnp.zeros((2, 6, half), cmp_pos.dtype)], axis=1)
    return _bf(w1ab), _bf(cmp_w2), _bf(posrows)


def _cmp_prompt_kernel(x_ref, w1ab_ref, w2_ref, posrows_ref, o_ref):
    n_chunks = x_ref.shape[0] // CMP_STRIDE
    xc = jnp.concatenate(
        [_bf(x_ref[pl.ds(j, n_chunks, stride=CMP_STRIDE), :]) for j in range(CMP_STRIDE)], axis=-1)
    o_ref[...] = _compress_tokens(xc, w1ab_ref, w2_ref, posrows_ref)


def nsa_compress_prompt(z, b, t, cw):
    w1ab, w2, posrows = cw
    n_chunks = t // CMP_STRIDE
    kind_spec = lambda a: pl.BlockSpec((None,) + a.shape[1:], lambda i, c: (c // NSA_KVH, 0, 0))
    return pl.pallas_call(
        _cmp_prompt_kernel,
        grid=(b, 2 * NSA_KVH),
        in_specs=[pl.BlockSpec((t, HEAD_DIM), lambda i, c: (i, OFF_NKV // HEAD_DIM + c)),
                  kind_spec(w1ab), kind_spec(w2), kind_spec(posrows)],
        out_specs=pl.BlockSpec((None, None, n_chunks, HEAD_DIM), lambda i, c: (i, c, 0, 0)),
        out_shape=jax.ShapeDtypeStruct((b, 2 * NSA_KVH, n_chunks, HEAD_DIM), jnp.float32),
        compiler_params=_cparams(("parallel", "parallel")),
        name="nsa_compress_prompt",
    )(z, w1ab, w2, posrows)


NSA_TQ = 128
NSA_KEY_SPANS = 4


def _nsa_prompt_kernel(q_ref, ckv_ref, ks_ref, vs_ref, kw_ref, vw_ref, gate_ref, ovl_ref, exp_ref, o_ref,
                       ks_bf, vs_bf, kw_bf, vw_bf, oslc_sc, *, t):
    qt = pl.program_id(1)
    q0 = pl.multiple_of(qt * NSA_TQ, NSA_TQ)
    n_cmp = ckv_ref.shape[1]
    n_sel = t // SEL_BLOCK
    span = WINDOW + NSA_TQ
    nqt = t // NSA_TQ
    n_span = min(NSA_KEY_SPANS, nqt)

    @pl.when(qt == 0)
    def _():
        ks_bf[...] = _bf(ks_ref[...])
        vs_bf[...] = _bf(vs_ref[...])
        kw_bf[...] = _bf(kw_ref[...])
        vw_bf[...] = _bf(vw_ref[...])

    qpos = q0 + lax.broadcasted_iota(jnp.int32, (NSA_TQ, 1), 0)
    gates = jax.nn.sigmoid(gate_ref[...])
    lane = lax.broadcasted_iota(jnp.int32, (NSA_TQ, 128), 1)
    lane_c = lax.broadcasted_iota(jnp.int32, (NSA_TQ, n_cmp), 1)
    cmask = (lane_c * CMP_STRIDE + (CMP_LEN - 1) <= qpos) & (lane_c < n_cmp - 1)
    cur = qpos // SEL_BLOCK
    forced = (lane == 0) | (lane == cur) | (lane == cur - 1)
    wstart =pl.multiple_of(jnp.maximum(q0 - WINDOW, 0), NSA_TQ)
    wd = qpos - (wstart + lax.broadcasted_iota(jnp.int32, (NSA_TQ, span), 1))
    wmask = (wd >= 0) & (wd <= WINDOW)

    def attend(qs, k_bf, v_bf, mask):
        nk = k_bf.shape[0]
        s = _dot_nt(qs, k_bf).reshape(NSA_REP, NSA_TQ, nk) * SCALE
        p = _masked_softmax(s, mask[None])
        return p, _dot(_bf(p.reshape(NSA_REP * NSA_TQ, nk)), v_bf)

    for g in range(NSA_KVH):
        qs = _bf(jnp.concatenate(
            [q_ref[:, (g * NSA_REP + r) * HEAD_DIM:(g * NSA_REP + r + 1) * HEAD_DIM] for r in range(NSA_REP)],
            axis=0))
        p_cmp, o_cmp = attend(qs, _bf(ckv_ref[g]), _bf(ckv_ref[NSA_KVH + g]), cmask)
        psum = p_cmp[0] + p_cmp[1] + p_cmp[2] + p_cmp[3]
        imp = _dot_f32_lhs(psum, ovl_ref[...])
        score = jnp.where(lane <= cur, imp + jnp.where(forced, FORCE_BONUS, 0.0), NEG)
        rank = _rank_lanes(score, n_sel)
        n_ok = jnp.minimum(cur + 1, min(SEL_TOPK, n_sel)).astype(jnp.float32)
        sel = jnp.where(rank < n_ok, 1.0, 0.0)
        sel_b = _bf(sel)
        gsl = slice(g * HEAD_DIM, (g + 1) * HEAD_DIM)

        def selected(nk, sel_b=sel_b, qs=qs, gsl=gsl):
            keysel = _dot(sel_b, exp_ref[:, 0:nk])
            kpos = lax.broadcasted_iota(jnp.int32, (NSA_TQ, nk), 1)
            smask = jnp.where(kpos <= qpos, keysel, 0.0) > 0.5
            oslc_sc[...] = attend(qs, ks_bf[0:nk, gsl], vs_bf[0:nk, gsl], smask)[1]

        for j in range(n_span):
            pl.when(qt // (nqt // n_span) == j)(functools.partial(selected, (j + 1) * (t // n_span)))
        o_slc = oslc_sc[...]
        _, o_win = attend(qs, kw_bf[pl.ds(wstart, span), gsl], vw_bf[pl.ds(wstart, span), gsl], wmask)
        for r in range(NSA_REP):
            h = g * NSA_REP + r
            rows = slice(r * NSA_TQ, (r + 1) * NSA_TQ)
            o = (gates[:, h:h + 1] * o_cmp[rows] + gates[:, NSA_H + h:NSA_H + h + 1] * o_slc[rows]
                 + gates[:, 2 * NSA_H + h:2 * NSA_H + h + 1] * o_win[rows])
            o_ref[:, h * HEAD_DIM:(h + 1) * HEAD_DIM] = o.astype(o_ref.dtype)


def nsa_prompt(z, ckv, b, t):
    n_cmp = ckv.shape[2]
    n_sel = t // SEL_BLOCK
    nqt = t // NSA_TQ
    gw = NSA_KVH * HEAD_DIM
    cmp_start = np.arange(n_cmp) * CMP_STRIDE
    sel_start = np.arange(128) * SEL_BLOCK
    ovl = ((cmp_start[:, None] < sel_start[None, :] + SEL_BLOCK) & (cmp_start[:, None] + CMP_LEN > sel_start[None, :])
           & (np.arange(n_cmp)[:, None] < n_cmp - 1) & (np.arange(128)[None, :] < n_sel))
    ovl = jnp.asarray(ovl, jnp.bfloat16)
    expand = jnp.asarray(np.arange(128)[:, None] == (np.arange(t)[None, :] // SEL_BLOCK), jnp.bfloat16)
    kv_spec = lambda kind: pl.BlockSpec((t, gw), lambda i, j: (i, OFF_NKV // gw + kind))
    return pl.pallas_call(
        functools.partial(_nsa_prompt_kernel, t=t),
        grid=(b, nqt),
        in_specs=[pl.BlockSpec((NSA_TQ, N_NQ), lambda i, j: (i * nqt + j, OFF_NQ // N_NQ)),
                  pl.BlockSpec((None, 2 * NSA_KVH, n_cmp, HEAD_DIM), lambda i, j: (i, 0, 0, 0)),
                  kv_spec(2), kv_spec(3), kv_spec(4), kv_spec(5),
                  pl.BlockSpec((NSA_TQ, 128), lambda i, j: (i * nqt + j, OFF_NGATE // 128)),
                  pl.BlockSpec(ovl.shape, lambda i, j: (0, 0)),
                  pl.BlockSpec(expand.shape, lambda i, j: (0, 0))],
        out_specs=pl.BlockSpec((NSA_TQ, MIX_W), lambda i, j: (i * nqt + j, 0)),
        out_shape=jax.ShapeDtypeStruct((b * t, MIX_W), jnp.bfloat16),
        scratch_shapes=[pltpu.VMEM((t, gw), jnp.bfloat16)] * 4
        + [pltpu.VMEM((NSA_REP * NSA_TQ, HEAD_DIM), jnp.float32)],
        compiler_params=_cparams(("parallel", "arbitrary")),
        name="nsa_prompt",
    )(z, ckv, z, z, z, z, z, ovl, expand)


def _moba_prompt_kernel(q_ref, k_ref, v_ref, exp_ref, o_ref, k_bf, v_bf, km_ref, *, t):
    qt = pl.program_id(2)
    nb = t // MOBA_BLOCK

    @pl.when(qt == 0)
    def _():
        k = k_ref[...]
        k_bf[...] = _bf(k)
        v_bf[...] = _bf(v_ref[...])
        km_ref[...] = jnp.zeros(km_ref.shape, jnp.float32)
        km_ref[0:nb, :] = jnp.mean(k.reshape(nb, MOBA_BLOCK, HEAD_DIM), axis=1)

    q = q_ref[...]
    qpos = qt * MOBA_BLOCK + lax.broadcasted_iota(jnp.int32, (MOBA_BLOCK, 1), 0)
    lane = lax.broadcasted_iota(jnp.int32, (MOBA_BLOCK, 128), 1)
    gate = jnp.where(lane < qt, _dot_nt_f32(q, km_ref[...]), NEG)
    rank = _rank_lanes(gate, nb)
    n_ok = jnp.minimum(qt, min(MOBA_TOPK, nb)).astype(jnp.float32)
    sel = _bf(jnp.where(rank < n_ok, 1.0, 0.0))
    q_bf = _bf(q)

    def attend(nk):
        keysel = _dot(sel, exp_ref[:, 0:nk])
        kpos = lax.broadcasted_iota(jnp.int32, (MOBA_BLOCK, nk), 1)
        own = (kpos // MOBA_BLOCK == qt) & (kpos <= qpos)
        mask = jnp.where(own, 1.0, keysel) > 0.5
        s = _dot_nt(q_bf, k_bf[0:nk, :]) * SCALE
        p = _masked_softmax(s, mask)
        o_ref[...] = _dot(_bf(p), v_bf[0:nk, :]).astype(o_ref.dtype)

    for j in range(nb):
        pl.when(qt == j)(functools.partial(attend, (j + 1) * MOBA_BLOCK))


def moba_prompt(z, b, t):
    nqt = t // MOBA_BLOCK
    expand = jnp.asarray(np.arange(128)[:, None] == (np.arange(t)[None, :] // MOBA_BLOCK), jnp.bfloat16)
    col = lambda part: OFF_MQKV // HEAD_DIM + part * MOBA_H
    return pl.pallas_call(
        functools.partial(_moba_prompt_kernel, t=t),
        grid=(b, MOBA_H, nqt),
        in_specs=[pl.BlockSpec((MOBA_BLOCK, HEAD_DIM), lambda i, h, j: (i * nqt + j, col(0) + h)),
                  pl.BlockSpec((t, HEAD_DIM), lambda i, h, j: (i, col(1) + h)),
                  pl.BlockSpec((t, HEAD_DIM), lambda i, h, j: (i, col(2) + h)),
                  pl.BlockSpec(expand.shape, lambda i, h, j: (0, 0))],
        out_specs=pl.BlockSpec((MOBA_BLOCK, HEAD_DIM), lambda i, h, j: (i * nqt + j, h)),
        out_shape=jax.ShapeDtypeStruct((b * t, MIX_W), jnp.bfloat16),
        scratch_shapes=[pltpu.VMEM((t, HEAD_DIM), jnp.bfloat16), pltpu.VMEM((t, HEAD_DIM), jnp.bfloat16),
                        pltpu.VMEM((128, HEAD_DIM), jnp.float32)],
        compiler_params=_cparams(("parallel", "parallel", "arbitrary")),
        name="moba_prompt",
    )(z, z, z, expand)


HG_KCHUNK = 128
HG_SUB = 16
HG_HEADS_PER_STEP = 2


def _hgrn_gates(hq, hf, lb):
    q = jax.nn.silu(hq)
    f = lb + (1.0 - lb) * jax.nn.sigmoid(hf)
    logf = jnp.log(jnp.maximum(f, F_FLOOR))
    k = (1.0 - lb) * jax.nn.sigmoid(-hf)
    return q, k, logf


def _hgrn_out(o, hg, norm_w):
    o = o * lax.rsqrt(jnp.mean(o * o, axis=-1, keepdims=True) + EPS)
    return o * norm_w * jax.nn.silu(hg)


def _hgrn_prompt_kernel(hq_ref, hf_ref, hi_ref, hg_ref, lb_ref, nw_ref, tri_ref, o_ref, s_ref,
                        st_ref, kcv_sc, *, t):
    c = HG_KCHUNK
    st_ref[...] = jnp.zeros(st_ref.shape, jnp.float32)
    row_s = lax.broadcasted_iota(jnp.int32, (HG_SUB, 1), 0)
    col_c = lax.broadcasted_iota(jnp.int32, (HG_SUB, c), 1)

    def head_chunk(hh, rows):
        lanes = slice(hh * HG_DK, (hh + 1) * HG_DK)
        lb = lb_ref[:, lanes]
        nw = nw_ref[:, lanes]
        k_sc, cum_sc, v_sc = kcv_sc.at[hh, 0], kcv_sc.at[hh, 1], kcv_sc.at[hh, 2]
        q, k, logf = _hgrn_gates(hq_ref[rows, lanes], hf_ref[rows, lanes], lb)
        v = hi_ref[rows, lanes]
        cum = _dot_f32_rhs(tri_ref[...], logf)
        k_sc[...] = k
        cum_sc[...] = cum
        v_sc[...] = v
        v_b = _bf(v)
        st = st_ref[hh]
        o_carry = _dot_nt(_bf(q * jnp.exp(cum)), _bf(st))
        o_parts = []
        for i in range(c // HG_SUB):
            r0 = i * HG_SUB
            q_i = q[r0:r0 + HG_SUB]
            cum_i = cum[r0:r0 + HG_SUB]
            o_i = o_carry[r0:r0 + HG_SUB]
            if i > 0:
                edge = cum[r0 - 1:r0]
                k_e = k * jnp.exp(jnp.minimum(edge - cum, 0.0))
                a = _dot_nt(_bf(q_i * jnp.exp(cum_i - edge)), _bf(k_e))
                o_i = o_i + _dot(_bf(jnp.where(col_c < r0, a, 0.0)), v_b)
            for s in range(r0, r0 + HG_SUB):
                d = jnp.minimum(cum_i - cum_sc[pl.ds(s, 1), :], 0.0)
                w = jnp.sum(q_i * jnp.exp(d) * k_sc[pl.ds(s, 1), :], axis=-1, keepdims=True)
                o_i = o_i + jnp.where(row_s >= s - r0, w, 0.0) * v_sc[pl.ds(s, 1), :]
            o_parts.append(o_i)
        o = jnp.concatenate(o_parts, axis=0)
        o_ref[rows, lanes] = _hgrn_out(o, hg_ref[rows, lanes], nw).astype(o_ref.dtype)
        last = cum[c - 1:c]
        kd = k * jnp.exp(last - cum)
        upd = lax.dot_general(v_b, _bf(kd), (((0,), (0,)), ((), ())), preferred_element_type=jnp.float32)
        st_ref[hh] = st * jnp.exp(last) + upd

    def chunk(ci, carry):
        rows = pl.ds(pl.multiple_of(ci * c, c), c)
        for hh in range(HG_HEADS_PER_STEP):
            head_chunk(hh, rows)
        return carry

    lax.fori_loop(0, t // c, chunk, 0)
    for hh in range(HG_HEADS_PER_STEP):
        s_ref[hh] = st_ref[hh].T


def hgrn_prompt(z, lb, norm_w, b, t):
    tri = jnp.asarray(np.tril(np.ones((HG_KCHUNK, HG_KCHUNK))), jnp.bfloat16)
    hp = HG_HEADS_PER_STEP
    width = hp * HG_DK
    col = lambda off: (lambda i, h: (i, off // width + h))
    vec = pl.BlockSpec((1, width), lambda i, h: (0, h))
    return pl.pallas_call(
        functools.partial(_hgrn_prompt_kernel, t=t),
        grid=(b, HG_H // hp),
        in_specs=[pl.BlockSpec((t, width), col(OFF_HQ)), pl.BlockSpec((t, width), col(OFF_HF)),
                  pl.BlockSpec((t, width), col(OFF_HI)), pl.BlockSpec((t, width), col(OFF_HGATE)),
                  vec, vec, pl.BlockSpec(tri.shape, lambda i, h: (0, 0))],
        out_specs=[pl.BlockSpec((t, width), lambda i, h: (i, h)),
                   pl.BlockSpec((None, hp, HG_DK, HG_DV), lambda i, h: (i, h, 0, 0))],
        out_shape=[jax.ShapeDtypeStruct((b * t, MIX_W), jnp.bfloat16),
                   jax.ShapeDtypeStruct((b, HG_H, HG_DK, HG_DV), jnp.float32)],
        scratch_shapes=[pltpu.VMEM((hp, HG_DV, HG_DK), jnp.float32),
                        pltpu.VMEM((hp, 3, HG_KCHUNK, HG_DK), jnp.float32)],
        compiler_params=_cparams(("parallel", "parallel")),
        name="hgrn_prompt",
    )(z, z, z, z, lb.reshape(1, MIX_W), norm_w.reshape(1, MIX_W), tri)


SEL_ROWS = 40


def _qbd(q, heads_per_kv):
    n_kv = q.shape[0] // heads_per_kv
    row = lax.broadcasted_iota(jnp.int32, q.shape, 0)
    blocks = [jnp.where(row // heads_per_kv == g, q, 0.0) for g in range(n_kv)]
    top = jnp.concatenate(blocks, axis=1)
    return jnp.concatenate([top, jnp.zeros((128 - q.shape[0], top.shape[1]), jnp.float32)], axis=0)


def _softmax_keys(s_sc, s_new, new_ok):
    s = s_sc[...]
    s_new = jnp.where(new_ok, s_new, NEG)
    m = jnp.maximum(jnp.max(s, axis=0, keepdims=True), s_new)
    e = jnp.where(s > 0.5 * NEG, jnp.exp(s - m), 0.0)
    e_new = jnp.where(new_ok, jnp.exp(s_new - m), 0.0)
    l = jnp.sum(e, axis=0, keepdims=True) + e_new
    inv = 1.0 / jnp.where(l > 0.0, l, 1.0)
    return e * inv, e_new * inv


def _weighted_values(p_page, v_of, acc, heads_per_kv):
    rows = p_page.shape[0]
    out = []
    for h in range(len(acc)):
        prod = p_page[:, h:h + 1] * v_of(h // heads_per_kv)
        out.append(acc[h] + jnp.sum(prod.reshape(rows // 8, 8, HEAD_DIM), axis=0))
    return out


def _finish_values(acc, p_new, v_new, heads_per_kv):
    rows = []
    for h in range(len(acc)):
        g = h // heads_per_kv
        rows.append(jnp.sum(acc[h], axis=0, keepdims=True) + p_new[:, h:h + 1] * v_new[:, g * HEAD_DIM:(g + 1) * HEAD_DIM])
    return jnp.concatenate(rows, axis=0)


def _cmp_decode_kernel(pt_ref, *refs, n_pages, pos):
    pages = refs[:n_pages]
    q_ref, w1ab_ref, w2_ref, posrows_ref, ovlt_ref, grp_ref, o_ref, sel_ref, xs = refs[n_pages:]
    n_tok = n_pages * CMP_PER_PAGE
    for pp in range(n_pages // 2):
        for j in range(CMP_STRIDE):
            for c in range(2 * NSA_KVH):
                piece = jnp.concatenate(
                    [_stream(pages[2 * pp + i], c, j, CMP_PER_PAGE, CMP_STRIDE) for i in range(2)], axis=0)
                xs[c, 2 * CMP_PER_PAGE * pp:2 * CMP_PER_PAGE * (pp + 1), j * HEAD_DIM:(j + 1) * HEAD_DIM] = _bf(piece)
    kc = [_compress_tokens(xs[g], w1ab_ref.at[0], w2_ref.at[0], posrows_ref.at[0]) for g in range(NSA_KVH)]
    vc = [_compress_tokens(xs[NSA_KVH + g], w1ab_ref.at[1], w2_ref.at[1], posrows_ref.at[1])
          for g in range(NSA_KVH)]
    qbd = _bf(_qbd(q_ref[...], NSA_REP))
    s = _dot_nt(_bf(jnp.concatenate(kc, axis=1)), qbd) * SCALE
    tok = lax.broadcasted_iota(jnp.int32, s.shape, 0)
    p = _masked_softmax_rows(s, (tok * CMP_STRIDE + (CMP_LEN - 1) <= pos) & (tok < n_tok - 1))
    o_ref[...] = jnp.concatenate(
        [jnp.sum(p[:, h:h + 1] * vc[h // NSA_REP], axis=0, keepdims=True) for h in range(NSA_H)], axis=0)
    imp = _dot_f32_rhs(ovlt_ref[...], _dot_f32_lhs(p, grp_ref[...]))
    blk = lax.broadcasted_iota(jnp.int32, imp.shape, 0)
    cur = pos // SEL_BLOCK
    forced = (blk == 0) | (blk == cur) | (blk == cur - 1)
    score = jnp.where(blk <= cur, imp + jnp.where(forced, FORCE_BONUS, 0.0), NEG)
    n_sel = cur + 1
    rank = _rank_sublanes(score, n_sel)
    sel_ref[...] = jnp.where(rank < float(min(SEL_TOPK, n_sel)), 1.0, 0.0)


def _page_specs(n_pages, layer, heads, phased=False):
    def spec(p):
        if phased:
            return pl.BlockSpec((None, None, PAGE_SIZE, None, heads, HEAD_DIM),
                                lambda b, ph, pt: (layer, pt[b * n_pages + p], 0, ph, 0, 0))
        return pl.BlockSpec((None, None, PAGE_SIZE, 2, heads, HEAD_DIM),
                            lambda b, pt: (layer, pt[b * n_pages + p], 0, 0, 0, 0))
    return [spec(p) for p in range(n_pages)]


def _stream(ref, idx, first=0, count=None, step=1):
    rows = ref.shape[0]
    streams = int(np.prod(ref.shape[1:-1]))
    count = rows if count is None else count
    flat = ref.reshape(rows * streams, ref.shape[-1])
    return flat[pl.ds(first * streams + idx, count, stride=step * streams), :]


def _heads_on_lanes(ref, kind, heads):
    return jnp.concatenate([_stream(ref, kind * heads + h) for h in range(heads)], axis=1)


def nsa_cmp_decode(cache_cmp, layer, pt_flat, q_s, cw, n_pages):
    w1ab, w2, posrows = cw
    b = q_s.shape[0]
    pos = n_pages * PAGE_SIZE
    n_tok = n_pages * CMP_PER_PAGE
    cmp_start = np.arange(n_tok) * CMP_STRIDE
    sel_start = np.arange(SEL_ROWS) * SEL_BLOCK
    ovlt = ((cmp_start[None, :] < sel_start[:, None] + SEL_BLOCK) & (cmp_start[None, :] + CMP_LEN > sel_start[:, None])
            & (np.arange(n_tok)[None, :] < n_tok - 1))
    hh = np.arange(128)
    grp = (hh[:, None] // NSA_REP == hh[None, :] // NSA_REP) & (hh[:, None] < NSA_H) & (hh[None, :] < NSA_H)
    ovlt, grp = jnp.asarray(ovlt, jnp.bfloat16), jnp.asarray(grp, jnp.bfloat16)
    const = lambda a: pl.BlockSpec(a.shape, lambda i, pt: (0,) * a.ndim)
    grid_spec = pltpu.PrefetchScalarGridSpec(
        num_scalar_prefetch=1, grid=(b,),
        in_specs=_page_specs(n_pages, layer, NSA_KVH)
        + [pl.BlockSpec((None, NSA_H, HEAD_DIM), lambda i, pt: (i, 0, 0)),
           const(w1ab), const(w2), const(posrows), const(ovlt), const(grp)],
        out_specs=[pl.BlockSpec((None, NSA_H, HEAD_DIM), lambda i, pt: (i, 0, 0)),
                   pl.BlockSpec((None, SEL_ROWS, 128), lambda i, pt: (i, 0, 0))],
        scratch_shapes=[pltpu.VMEM((2 * NSA_KVH, n_tok, CMP_STRIDE * HEAD_DIM), jnp.bfloat16)])
    return pl.pallas_call(
        functools.partial(_cmp_decode_kernel, n_pages=n_pages, pos=pos),
        grid_spec=grid_spec,
        out_shape=[jax.ShapeDtypeStruct((b, NSA_H, HEAD_DIM), jnp.float32),
                   jax.ShapeDtypeStruct((b, SEL_ROWS, 128), jnp.float32)],
        compiler_params=_cparams(("parallel",)),
        name="nsa_cmp_decode",
    )(pt_flat, *([cache_cmp] * n_pages), q_s, w1ab, w2, posrows, ovlt, grp)


def _slc_decode_kernel(pt_ref, *refs, n_pages, pos):
    pages = refs[:n_pages]
    q_ref, kv_ref, sel_ref, o_ref, s_sc = refs[n_pages:]
    kw = NSA_KVH * HEAD_DIM
    qbd = _bf(_qbd(q_ref[...], NSA_REP))
    row = lax.broadcasted_iota(jnp.int32, (PAGE_SIZE, 128), 0)
    per_page = PAGE_SIZE // SEL_BLOCK
    for p in range(n_pages):
        s = _dot_nt(_bf(_heads_on_lanes(pages[p], 0, NSA_KVH)), qbd) * SCALE
        sel = sel_ref[pl.ds(per_page * p, 1), :]
        for i in range(1, per_page):
            sel = jnp.where(row >= i * SEL_BLOCK, sel_ref[pl.ds(per_page * p + i, 1), :], sel)
        kpos = p * PAGE_SIZE + row
        s_sc[p * PAGE_SIZE:(p + 1) * PAGE_SIZE, :] = jnp.where(kpos <= pos, jnp.where(sel > 0.5, s, NEG), NEG)
    k_new = kv_ref[:, 2 * kw:3 * kw]
    v_new = kv_ref[:, 3 * kw:4 * kw]
    s_new = _dot_nt(_bf(jnp.broadcast_to(k_new, (8, kw))), qbd)[0:1] * SCALE
    new_ok = sel_ref[pl.ds(pos // SEL_BLOCK, 1), :] > 0.5
    _softmax_keys_into(s_sc, s_new, new_ok, o_ref, pages, v_new, NSA_REP, PAGE_SIZE)


def _softmax_keys_into(s_sc, s_new, new_ok, o_ref, pages, v_new, heads_per_kv, rows):
    p_all, p_new = _softmax_keys(s_sc, s_new, new_ok)
    s_sc[...] = p_all
    n_heads = o_ref.shape[0]
    n_kv = n_heads // heads_per_kv
    acc = [jnp.zeros((n_heads, HEAD_DIM), jnp.float32)] * n_kv
    for p, page in enumerate(pages):
        p_bf = _bf(s_sc[p * rows:(p + 1) * rows, :])
        for g in range(n_kv):
            pv = lax.dot_general(p_bf, _bf(_stream(page, n_kv + g)), (((0,), (0,)), ((), ())),
                                 preferred_element_type=jnp.float32)
            acc[g] = acc[g] + pv[0:n_heads]
    head = lax.broadcasted_iota(jnp.int32, (n_heads, 1), 0)
    o = jnp.zeros((n_heads, HEAD_DIM), jnp.float32)
    for g in range(n_kv):
        new = jnp.concatenate([p_new[:, h:h + 1] for h in range(n_heads)], axis=0) * v_new[:, g * HEAD_DIM:(g + 1) * HEAD_DIM]
        o = o + jnp.where(head // heads_per_kv == g, acc[g] + new, 0.0)
    o_ref[...] = o.astype(o_ref.dtype)


def nsa_slc_decode(cache_slc, layer, pt_flat, q_s, kv_new, sel, n_pages):
    b = q_s.shape[0]
    pos = n_pages * PAGE_SIZE
    grid_spec = pltpu.PrefetchScalarGridSpec(
        num_scalar_prefetch=1, grid=(b,),
        in_specs=_page_specs(n_pages, layer, NSA_KVH)
        + [pl.BlockSpec((None, NSA_H, HEAD_DIM), lambda i, pt: (i, 0, 0)),
           pl.BlockSpec((None, 1, N_NKV), lambda i, pt: (i, 0, 0)),
           pl.BlockSpec((None, SEL_ROWS, 128), lambda i, pt: (i, 0, 0))],
        out_specs=pl.BlockSpec((None, NSA_H, HEAD_DIM), lambda i, pt: (i, 0, 0)),
        scratch_shapes=[pltpu.VMEM((n_pages * PAGE_SIZE, 128), jnp.float32)])
    return pl.pallas_call(
        functools.partial(_slc_decode_kernel, n_pages=n_pages, pos=pos),
        grid_spec=grid_spec,
        out_shape=jax.ShapeDtypeStruct((b, NSA_H, HEAD_DIM), jnp.float32),
        compiler_params=_cparams(("parallel",)),
        name="nsa_slc_decode",
    )(pt_flat, *([cache_slc] * n_pages), q_s, kv_new, sel)


def _win_decode_kernel(win_ref, q_ref, kv_ref, ocmp_ref, oslc_ref, gate_ref, *rest, pos, aliased):
    o_ref, wout_ref, s_sc, ow_sc = rest[1:] if aliased else rest
    kw = NSA_KVH * HEAD_DIM
    wb = win_ref.shape[0]
    wout_ref[pl.ds(0, wb - 1)] = win_ref[pl.ds(1, wb - 1)]
    for kind in range(2):
        for g in range(NSA_KVH):
            lane0 = (4 + kind) * kw + g * HEAD_DIM
            wout_ref[wb - 1, kind, pl.ds(g, 1), :] = kv_ref[:, lane0:lane0 + HEAD_DIM]
    qbd = _bf(_qbd(q_ref[...], NSA_REP))
    s = _dot_nt(_bf(_heads_on_lanes(win_ref, 0, NSA_KVH)), qbd) * SCALE
    dist = pos - (pos - wb + lax.broadcasted_iota(jnp.int32, s.shape, 0))
    s_sc[...] = jnp.where(dist >= 0, jnp.where(dist <= WINDOW, s, NEG), NEG)
    k_new = kv_ref[:, 4 * kw:5 * kw]
    v_new = kv_ref[:, 5 * kw:6 * kw]
    s_new = _dot_nt(_bf(jnp.broadcast_to(k_new, (8, kw))), qbd)[0:1] * SCALE
    new_ok = jnp.ones(s_new.shape, jnp.float32) > 0.5
    _softmax_keys_into(s_sc, s_new, new_ok, ow_sc, [win_ref], v_new, NSA_REP, wb)
    g = jax.nn.sigmoid(gate_ref[...])
    o_ref[...] = (g[0] * ocmp_ref[...] + g[1] * oslc_ref[...] + g[2] * ow_sc[...]).astype(o_ref.dtype)


def _layer_slot(prev):
    if prev is None:
        return [], [], False
    return [prev], [pl.BlockSpec(memory_space=pl.ANY)], True


def nsa_win_decode(cache_win, layer, q_s, kv_new, o_cmp, o_slc, gate_cols, pos, win_out_prev):
    b, wb = cache_win.shape[1], cache_win.shape[2]
    hspec = pl.BlockSpec((None, NSA_H, HEAD_DIM), lambda i: (i, 0, 0))
    wspec = pl.BlockSpec((None, None, wb, 2, NSA_KVH, HEAD_DIM), lambda i: (layer, i, 0, 0, 0, 0))
    extra, extra_specs, aliased = _layer_slot(win_out_prev)
    return pl.pallas_call(
        functools.partial(_win_decode_kernel, pos=pos, aliased=aliased),
        grid=(b,),
        in_specs=[wspec, hspec, pl.BlockSpec((None, 1, N_NKV), lambda i: (i, 0, 0)), hspec, hspec,
                  pl.BlockSpec((None, 3, NSA_H, 1), lambda i: (i, 0, 0, 0))] + extra_specs,
        out_specs=[hspec, wspec],
        out_shape=[jax.ShapeDtypeStruct((b, NSA_H, HEAD_DIM), jnp.bfloat16),
                   jax.ShapeDtypeStruct(cache_win.shape, cache_win.dtype)],
        scratch_shapes=[pltpu.VMEM((wb, 128), jnp.float32), pltpu.VMEM((NSA_H, HEAD_DIM), jnp.float32)],
        input_output_aliases={6: 1} if aliased else {},
        compiler_params=_cparams(("parallel",)),
        name="nsa_win_decode",
    )(cache_win, q_s, kv_new, o_cmp, o_slc, gate_cols, *extra)


def _moba_decode_kernel(pt_ref, *refs, n_pages, pos):
    pages = refs[:n_pages]
    q_ref, kn_ref, vn_ref, o_ref, s_sc, pn_sc = refs[n_pages:]
    phase = pl.program_id(1)
    per_blk = MOBA_BLOCK // PAGE_SIZE
    nb = n_pages // per_blk

    @pl.when(phase == 0)
    def _():
        qf = _qbd(q_ref[...], 1)
        qbd = _bf(qf)
        ksum = []
        for p in range(n_pages):
            k = jnp.concatenate([_stream(pages[p], h) for h in range(MOBA_H)], axis=1)
            s_sc[p * PAGE_SIZE:(p + 1) * PAGE_SIZE, :] = _dot_nt(_bf(k), qbd) * SCALE
            ksum.append(jnp.sum(k, axis=0, keepdims=True))
        kmean = jnp.concatenate(
            [sum(ksum[n * per_blk:(n + 1) * per_blk]) * (1.0 / MOBA_BLOCK) for n in range(nb)], axis=0)
        blk = lax.broadcasted_iota(jnp.int32, (nb, 128), 0)
        cur = pos // MOBA_BLOCK
        gate = jnp.where(blk < cur, _dot_nt_f32(kmean, qf), NEG)
        rank = _rank_sublanes(gate, nb)
        sel = jnp.where(rank < float(min(cur, MOBA_TOPK, nb)), 1.0, 0.0)
        row = lax.broadcasted_iota(jnp.int32, (PAGE_SIZE, 128), 0)
        for p in range(n_pages):
            ok = (jnp.broadcast_to(sel[p // per_blk:p // per_blk + 1, :], (PAGE_SIZE, 128)) > 0.5)
            kpos = p * PAGE_SIZE + row
            rows = slice(p * PAGE_SIZE, (p + 1) * PAGE_SIZE)
            s_sc[rows, :] = jnp.where(kpos <= pos, jnp.where(ok, s_sc[rows, :], NEG), NEG)
        s_new = _dot_nt(_bf(jnp.broadcast_to(kn_ref[...], (8, MIX_W))), qbd)[0:1] * SCALE
        new_ok = jnp.ones(s_new.shape, jnp.float32) > 0.5
        p_all, p_new = _softmax_keys(s_sc, s_new, new_ok)
        s_sc[...] = p_all
        pn_sc[...] = jnp.broadcast_to(p_new, pn_sc.shape)

    @pl.when(phase == 1)
    def _():
        acc = jnp.zeros((MOBA_H, MIX_W), jnp.float32)
        for p in range(n_pages):
            v_all = _bf(jnp.concatenate([_stream(pages[p], h) for h in range(MOBA_H)], axis=1))
            pv = lax.dot_general(_bf(s_sc[p * PAGE_SIZE:(p + 1) * PAGE_SIZE, :]), v_all, (((0,), (0,)), ((), ())),
                                 preferred_element_type=jnp.float32)
            acc = acc + pv[0:MOBA_H]
        p_new, v_new = pn_sc[0:1, :], vn_ref[...]
        o_ref[...] = jnp.concatenate(
            [acc[h:h + 1, h * HEAD_DIM:(h + 1) * HEAD_DIM]
             + p_new[:, h:h + 1] * v_new[:, h * HEAD_DIM:(h + 1) * HEAD_DIM] for h in range(MOBA_H)],
            axis=0).astype(o_ref.dtype)


def moba_decode(cache_moba, layer, pt_flat, q_m, k_new, v_new, n_pages):
    b = q_m.shape[0]
    pos = n_pages * PAGE_SIZE
    grid_spec = pltpu.PrefetchScalarGridSpec(
        num_scalar_prefetch=1, grid=(b, 2),
        in_specs=_page_specs(n_pages, layer, MOBA_H, phased=True)
        + [pl.BlockSpec((None, MOBA_H, HEAD_DIM), lambda i, ph, pt: (i, 0, 0)),
           pl.BlockSpec((None, 1, MIX_W), lambda i, ph, pt: (i, 0, 0)),
           pl.BlockSpec((None, 1, MIX_W), lambda i, ph, pt: (i, 0, 0))],
        out_specs=pl.BlockSpec((None, MOBA_H, HEAD_DIM), lambda i, ph, pt: (i, 0, 0)),
        scratch_shapes=[pltpu.VMEM((n_pages * PAGE_SIZE, 128), jnp.float32), pltpu.VMEM((8, 128), jnp.float32)])
    return pl.pallas_call(
        functools.partial(_moba_decode_kernel, n_pages=n_pages, pos=pos),
        grid_spec=grid_spec,
        out_shape=jax.ShapeDtypeStruct((b, MOBA_H, HEAD_DIM), jnp.bfloat16),
        compiler_params=_cparams(("parallel", "arbitrary")),
        name="moba_decode",
    )(pt_flat, *([cache_moba] * n_pages), q_m, k_new, v_new)


def _hgrn_decode_kernel(s_ref, hqf_ref, hi_ref, hg_ref, lb_ref, nw_ref, *rest, aliased):
    o_ref, so_ref = rest[1:] if aliased else rest
    mine = lax.broadcasted_iota(jnp.int32, hqf_ref.shape[2:], 1) == pl.program_id(0)
    col = lambda x: jnp.sum(jnp.where(mine, x, 0.0), axis=1, keepdims=True)
    for h in range(HG_H):
        lanes = slice(h * HG_DV, (h + 1) * HG_DV)
        q, k, logf = _hgrn_gates(col(hqf_ref[0, h]), col(hqf_ref[1, h]), lb_ref[h])
        s_new = jnp.exp(logf) * s_ref[h] + k * hi_ref[:, lanes]
        so_ref[h] = s_new
        o = jnp.sum(q * s_new, axis=0, keepdims=True)
        o_ref[:, lanes] = _hgrn_out(o, hg_ref[:, lanes], nw_ref[:, lanes]).astype(o_ref.dtype)


def hgrn_decode(state, layer, hqf_t, hi_row, hg_row, lb, norm_w, state_out_prev):
    b = state.shape[1]
    rowspec = pl.BlockSpec((None, 1, MIX_W), lambda i: (i, 0, 0))
    sspec = pl.BlockSpec((None, None, HG_H, HG_DK, HG_DV), lambda i: (layer, i, 0, 0, 0))
    extra, extra_specs, aliased = _layer_slot(state_out_prev)
    return pl.pallas_call(
        functools.partial(_hgrn_decode_kernel, aliased=aliased),
        grid=(b,),
        in_specs=[sspec, pl.BlockSpec(hqf_t.shape, lambda i: (0, 0, 0, 0)), rowspec, rowspec,
                  pl.BlockSpec((HG_H, HG_DK, 1), lambda i: (0, 0, 0)),
                  pl.BlockSpec((1, MIX_W), lambda i: (0, 0))] + extra_specs,
        out_specs=[rowspec, sspec],
        out_shape=[jax.ShapeDtypeStruct((b, 1, MIX_W), jnp.bfloat16),
                   jax.ShapeDtypeStruct(state.shape, jnp.float32)],
        input_output_aliases={6: 1} if aliased else {},
        compiler_params=_cparams(("parallel",)),
        name="hgrn_decode",
    )(state, hqf_t, hi_row, hg_row, lb.reshape(HG_H, HG_DK, 1), norm_w.reshape(1, MIX_W), *extra)


def masked_probs(s, mask):
    p = jax.nn.softmax(jnp.where(mask, s, NEG), axis=-1)
    return jnp.where(mask, p, 0.0)


def map_query_chunks(fn, *xs):
    n_q = xs[0].shape[0]
    qc = math.gcd(n_q, Q_CHUNK)
    n = n_q // qc
    blocks = tuple(a.reshape((n, qc) + a.shape[1:]) for a in xs)
    out = lax.map(lambda a: fn(*a), blocks)
    return out.reshape((n_q,) + out.shape[2:])


def gather_pages(pool, pages):
    rows = pool[pages]
    return rows.reshape((-1,) + pool.shape[2:])


def nsa_compress(x, pos_emb, w1, w2):
    n_blk = (x.shape[0] - CMP_LEN) // CMP_STRIDE + 1
    idx = np.arange(n_blk)[:, None] * CMP_STRIDE + np.arange(CMP_LEN)[None, :]
    blk = x[idx] + pos_emb[None, :, None, :]
    blk = blk.transpose(0, 2, 1, 3).reshape(n_blk, x.shape[1], CMP_LEN * HEAD_DIM)
    return jax.nn.silu(blk @ w1) @ w2


def nsa_seq(q, q_pos, kc, vc, ks, vs, cmp_pos, cmp_w1, cmp_w2):
    n_q, seq_len = q.shape[0], kc.shape[0]
    rep = NSA_H // NSA_KVH
    scale = HEAD_DIM ** -0.5
    qg = q.reshape(n_q, NSA_KVH, rep, HEAD_DIM)
    k_cmp = nsa_compress(kc, cmp_pos[0], cmp_w1[0], cmp_w2[0])
    v_cmp = nsa_compress(vc, cmp_pos[1], cmp_w1[1], cmp_w2[1])
    n_cmp = k_cmp.shape[0]
    cmp_start = np.arange(n_cmp, dtype=np.int32) * CMP_STRIDE
    s = jnp.einsum('qgrd,ngd->qgrn', qg, k_cmp, preferred_element_type=jnp.float32) * scale
    cmask = (cmp_start + CMP_LEN - 1)[None, :] <= q_pos[:, None]
    p_cmp = masked_probs(s, cmask[:, None, None, :])
    o_cmp = jnp.einsum('qgrn,ngd->qgrd', p_cmp.astype(v_cmp.dtype), v_cmp)
    n_sel = -(-seq_len // SEL_BLOCK)
    sel_start = np.arange(n_sel, dtype=np.int32) * SEL_BLOCK
    overlap = ((cmp_start[:, None] < sel_start[None, :] + SEL_BLOCK)
               & (cmp_start[:, None] + CMP_LEN > sel_start[None, :])).astype(np.float32)
    imp = jnp.einsum('qgrn,nj->qgj', p_cmp, overlap, precision=lax.Precision.HIGHEST)
    cur = q_pos // SEL_BLOCK
    blk = np.arange(n_sel, dtype=np.int32)[None, :]
    valid = blk <= cur[:, None]
    forced = (blk == 0) | (blk == cur[:, None]) | (blk == cur[:, None] - 1)
    score = jnp.where(valid[:, None, :],
                      imp + jnp.where(forced, FORCE_BONUS, 0.0)[:, None, :], NEG)
    n_top = min(SEL_TOPK, n_sel)
    _, idx = lax.top_k(score, n_top)
    ok = np.arange(n_top, dtype=np.int32)[None, :] < jnp.minimum(cur + 1, n_top)[:, None]
    pad = n_sel * SEL_BLOCK - seq_len
    ksb = jnp.pad(ks, ((0, pad), (0, 0), (0, 0))).reshape(
        n_sel, SEL_BLOCK, NSA_KVH, HEAD_DIM).transpose(2, 0, 1, 3)
    vsb = jnp.pad(vs, ((0, pad), (0, 0), (0, 0))).reshape(
        n_sel, SEL_BLOCK, NSA_KVH, HEAD_DIM).transpose(2, 0, 1, 3)
    g_ix = np.arange(NSA_KVH)[None, :, None]
    offs = np.arange(SEL_BLOCK, dtype=np.int32)

    def sel_chunk(q_c, pos_c, idx_c, ok_c):
        n = q_c.shape[0]
        k_g = ksb[g_ix, idx_c]
        v_g = vsb[g_ix, idx_c]
        k_pos = idx_c[..., None] * SEL_BLOCK + offs
        m = (k_pos <= pos_c[:, None, None, None]) & ok_c[:, None, :, None]
        s_c = jnp.einsum('qgrd,qgkbd->qgrkb', q_c, k_g, preferred_element_type=jnp.float32) * scale
        p_c = masked_probs(s_c.reshape(n, NSA_KVH, rep, -1), m.reshape(n, NSA_KVH, 1, -1))
        return jnp.einsum('qgrkb,qgkbd->qgrd', p_c.reshape(s_c.shape).astype(v_g.dtype), v_g)

    o_slc = map_query_chunks(sel_chunk, qg, q_pos, idx, ok)
    return o_cmp.reshape(n_q, NSA_H, HEAD_DIM), o_slc.reshape(n_q, NSA_H, HEAD_DIM)


def window_banded(q, k, v):
    b, t = q.shape[:2]
    rep = NSA_H // NSA_KVH
    qb = math.gcd(t, WIN_QBLOCK)
    nb = t // qb
    span = WINDOW + qb
    idx = np.arange(nb)[:, None] * qb + np.arange(span)[None, :]
    k_pos = idx - WINDOW
    q_pos = np.arange(t).reshape(nb, qb)
    d = q_pos[:, :, None] - k_pos[:, None, :]
    mask = (d >= 0) & (d <= WINDOW) & (k_pos[:, None, :] >= 0)
    kp = jnp.pad(k, ((0, 0), (WINDOW, 0), (0, 0), (0, 0)))[:, idx]
    vp = jnp.pad(v, ((0, 0), (WINDOW, 0), (0, 0), (0, 0)))[:, idx]
    qg = q.reshape(b, nb, qb, NSA_KVH, rep, HEAD_DIM)
    s = jnp.einsum('bnqgrd,bnkgd->bnqgrk', qg, kp, preferred_element_type=jnp.float32) * (HEAD_DIM ** -0.5)
    p = masked_probs(s, mask[None, :, :, None, None, :])
    o = jnp.einsum('bnqgrk,bnkgd->bnqgrd', p.astype(vp.dtype), vp)
    return o.reshape(b, t, NSA_H, HEAD_DIM)


def window_direct(q, q_pos, k, v, k_pos):
    b, n_q = q.shape[:2]
    rep = NSA_H // NSA_KVH
    qg = q.reshape(b, n_q, NSA_KVH, rep, HEAD_DIM)
    s = jnp.einsum('bqgrd,bkgd->bqgrk', qg, k, preferred_element_type=jnp.float32) * (HEAD_DIM ** -0.5)
    d = q_pos[:, None] - k_pos[None, :]
    mask = (d >= 0) & (d <= WINDOW)
    p = masked_probs(s, mask[None, :, None, None, :])
    o = jnp.einsum('bqgrk,bkgd->bqgrd', p.astype(v.dtype), v)
    return o.reshape(b, n_q, NSA_H, HEAD_DIM)


def nsa_combine(gate_logits, o_cmp, o_slc, o_win):
    b, t = o_cmp.shape[:2]
    g = jax.nn.sigmoid(gate_logits.astype(jnp.float32)).reshape(b, t, 3, NSA_H, 1)
    o = g[:, :, 0] * o_cmp + g[:, :, 1] * o_slc + g[:, :, 2] * o_win
    return o.reshape(b, t, MIX_W).astype(o_cmp.dtype)


def moba_seq(q, q_pos, k, v):
    n_q, seq_len = q.shape[0], k.shape[0]
    scale = HEAD_DIM ** -0.5
    nb = -(-seq_len // MOBA_BLOCK)
    pad = nb * MOBA_BLOCK - seq_len
    kb = jnp.pad(k, ((0, pad), (0, 0), (0, 0))).reshape(nb, MOBA_BLOCK, MOBA_H, HEAD_DIM)
    vb = jnp.pad(v, ((0, pad), (0, 0), (0, 0))).reshape(nb, MOBA_BLOCK, MOBA_H, HEAD_DIM)
    k_mean = jnp.mean(kb.astype(jnp.float32), axis=1)
    cur = q_pos // MOBA_BLOCK
    gate = jnp.einsum('qhd,nhd->qhn', q.astype(jnp.float32), k_mean, precision=lax.Precision.HIGHEST)
    past = np.arange(nb, dtype=np.int32)[None, :] < cur[:, None]
    gate = jnp.where(past[:, None, :], gate, NEG)
    n_top = min(MOBA_TOPK, nb)
    _, idx = lax.top_k(gate, n_top)
    ok = np.arange(n_top, dtype=np.int32)[None, :] < jnp.minimum(cur, n_top)[:, None]
    idx = jnp.concatenate([idx, jnp.broadcast_to(cur[:, None, None], (n_q, MOBA_H, 1))], axis=-1)
    ok = jnp.concatenate([ok, jnp.ones((n_q, 1), dtype=bool)], axis=-1)
    kbh = kb.transpose(2, 0, 1, 3)
    vbh = vb.transpose(2, 0, 1, 3)
    h_ix = np.arange(MOBA_H)[None, :, None]
    offs = np.arange(MOBA_BLOCK, dtype=np.int32)

    def chunk(q_c, pos_c, idx_c, ok_c):
        n = q_c.shape[0]
        k_g = kbh[h_ix, idx_c]
        v_g = vbh[h_ix, idx_c]
        k_pos = idx_c[..., None] * MOBA_BLOCK + offs
        m = (k_pos <= pos_c[:, None, None, None]) & ok_c[:, None, :, None]
        s_c = jnp.einsum('qhd,qhkbd->qhkb', q_c, k_g, preferred_element_type=jnp.float32) * scale
        p_c = masked_probs(s_c.reshape(n, MOBA_H, -1), m.reshape(n, MOBA_H, -1))
        return jnp.einsum('qhkb,qhkbd->qhd', p_c.reshape(s_c.shape).astype(v_g.dtype), v_g)

    return map_query_chunks(chunk, q, q_pos, idx, ok)


def hgrn_features(hq, hf, hi, lb):
    b, t, _ = hq.shape
    q = jax.nn.silu(hq.astype(jnp.float32)).reshape(b, t, HG_H, HG_DK)
    z = hf.astype(jnp.float32).reshape(b, t, HG_H, HG_DK)
    lbh = lb.astype(jnp.float32).reshape(HG_H, HG_DK)
    f = lbh + (1.0 - lbh) * jax.nn.sigmoid(z)
    logf = jnp.log(jnp.maximum(f, F_FLOOR))
    k = (1.0 - lbh) * jax.nn.sigmoid(-z)
    v = hi.astype(jnp.float32).reshape(b, t, HG_H, HG_DV)
    return q, k, logf, v


def hgrn_chunked(q, k, logf, v, s0):
    b, t, h, _ = q.shape
    c = math.gcd(t, HG_CHUNK)
    n = t // c
    tri = np.tril(np.ones((c, c), dtype=bool))[:, :, None]

    def blocks(a):
        return a.reshape(b, n, c, h, a.shape[-1]).transpose(1, 0, 3, 2, 4)

    def step(s, inp):
        q_c, k_c, l_c, v_c = inp
        cum = jnp.cumsum(l_c, axis=2)
        diff = cum[:, :, :, None, :] - cum[:, :, None, :, :]
        decay = jnp.where(tri, jnp.exp(jnp.where(tri, diff, 0.0)), 0.0)
        a = jnp.einsum('bhtd,bhsd,bhtsd->bhts', q_c, k_c, decay)
        o = jnp.einsum('bhts,bhse->bhte', a, v_c) + jnp.einsum('bhtd,bhde->bhte', q_c * jnp.exp(cum), s)
        last = cum[:, :, -1]
        s = jnp.exp(last)[..., None] * s + jnp.einsum(
            'bhsd,bhse->bhde', k_c * jnp.exp(last[:, :, None] - cum), v_c)
        return s, o

    s, o = lax.scan(step, s0, (blocks(q), blocks(k), blocks(logf), blocks(v)))
    return o.transpose(1, 0, 3, 2, 4).reshape(b, t, h, v.shape[-1]), s


def hgrn_recurrent(q, k, logf, v, s0):
    def step(s, inp):
        q_t, k_t, l_t, v_t = inp
        s = jnp.exp(l_t)[..., None] * s + k_t[..., None] * v_t[..., None, :]
        return s, jnp.einsum('bhd,bhde->bhe', q_t, s)

    sw = lambda a: jnp.swapaxes(a, 0, 1)
    s, o = lax.scan(step, s0, (sw(q), sw(k), sw(logf), sw(v)))
    return sw(o), s


def hgrn_output(o, hg, norm_w):
    b, t = o.shape[:2]
    o = o * lax.rsqrt(jnp.mean(o * o, axis=-1, keepdims=True) + EPS)
    o = o * norm_w.astype(jnp.float32).reshape(HG_H, HG_DV)
    g = jax.nn.silu(hg.astype(jnp.float32)).reshape(b, t, HG_H, HG_DV)
    return (o * g).reshape(b, t, MIX_W).astype(hg.dtype)


def split_z(z, b, t):
    z = z.reshape(b, t, D_IN_PAD)
    sl = lambda off, n: z[:, :, off:off + n]
    return (sl(OFF_NQ, N_NQ), sl(OFF_NKV, N_NKV), sl(OFF_NGATE, N_NGATE), sl(OFF_MQKV, N_MQKV),
            sl(OFF_HQ, N_HG), sl(OFF_HF, N_HG), sl(OFF_HI, N_HG), sl(OFF_HGATE, N_HG))


def finish_layer(x2, z, o_nsa, o_hg, o_moba, lw):
    layer = lw["layer"]
    mixed = branch_merge((o_nsa, o_hg, o_moba), lw["w_branch"], layer, z)
    x2 = matmul(mixed, lw["w_out"], layer, epilogue="residual", residual=x2, tn=512)
    h2 = rmsnorm_rows(x2, lw["norm2_w"], jnp.bfloat16)
    u2 = matmul(h2, lw["w_up"], layer, epilogue="relu2", out_dtype=jnp.bfloat16)
    return matmul(u2, lw["w_down"], layer, epilogue="residual", residual=x2, tn=512)


def prompt_layer(x, lb, lw):
    b, t, _ = x.shape
    x2 = x.reshape(b * t, D_MODEL)
    h = rmsnorm_rows(x2, lw["norm1_w"], jnp.bfloat16)
    z = project_in(h, lw["w_in_t"], lw["layer"])
    ckv = nsa_compress_prompt(z, b, t, lw["cw"])
    o_nsa = nsa_prompt(z, ckv, b, t)
    o_moba = moba_prompt(z, b, t)
    o_hg, s_fin = hgrn_prompt(z, lb, lw["hg_norm_w"], b, t)
    x2 = finish_layer(x2, z, o_nsa, o_hg, o_moba, lw)
    return (x2.reshape(b, t, D_MODEL),) + kv_cache_rows(z, b, t) + (s_fin,)


def kv_cache_rows(z, b, t):
    z3 = z.reshape(b, t, D_IN_PAD)
    gw = 2 * NSA_KVH * HEAD_DIM
    kv = lambda kind: z3[:, :, OFF_NKV + kind * gw:OFF_NKV + (kind + 1) * gw].reshape(b, t, 2, NSA_KVH, HEAD_DIM)
    mkv = z3[:, :, OFF_MQKV + MIX_W:OFF_MQKV + 3 * MIX_W].reshape(b, t, 2, MOBA_H, HEAD_DIM)
    return kv(0), kv(1), mkv, kv(2)[:, max(t - WINDOW, 0):]


def sample_layer(x, layer, caches, page_table, lb, lw, stacked_prev):
    b, t, _ = x.shape
    assert t == 1, "the decode kernels take one new token per sequence"
    x2 = x.reshape(b * t, D_MODEL)
    h = rmsnorm_rows(x2, lw["norm1_w"], jnp.bfloat16)
    z = project_in(h, lw["w_in_t"], lw["layer"])
    o_nsa, o_hg, o_moba, win_out, s_out = sample_mixers(z, layer, caches, page_table, lb, lw, stacked_prev)
    x2 = finish_layer(x2, z, o_nsa, o_hg, o_moba, lw)
    c_cmp, c_slc, c_moba, _ = kv_cache_rows(z, b, t)
    return (x2.reshape(b, t, D_MODEL), c_cmp, c_slc, c_moba), (win_out, s_out)


def sample_mixers(z, layer, caches, page_table, lb, lw, stacked_prev=(None, None)):
    cache_cmp, cache_slc, cache_moba, cache_win, state = caches
    win_prev, state_prev = stacked_prev
    b = z.shape[0]
    n_pages = page_table.shape[1]
    pos = n_pages * PAGE_SIZE
    pt_flat = page_table.reshape(-1)
    col = lambda off, n: z[:, off:off + n]
    q_s = col(OFF_NQ, N_NQ).reshape(b, NSA_H, HEAD_DIM)
    kv_new = col(OFF_NKV, N_NKV).reshape(b, 1, N_NKV)
    gate_cols = col(OFF_NGATE, N_NGATE).reshape(b, 3, NSA_H, 1)
    o_cmp, sel = nsa_cmp_decode(cache_cmp, layer, pt_flat, q_s, lw["cw"], n_pages)
    o_slc = nsa_slc_decode(cache_slc, layer, pt_flat, q_s, kv_new, sel, n_pages)
    o_nsa, win_out = nsa_win_decode(cache_win, layer, q_s, kv_new, o_cmp, o_slc, gate_cols, pos, win_prev)
    q_m = col(OFF_MQKV, MIX_W).reshape(b, MOBA_H, HEAD_DIM)
    k_new = col(OFF_MQKV + MIX_W, MIX_W).reshape(b, 1, MIX_W)
    v_new = col(OFF_MQKV + 2 * MIX_W, MIX_W).reshape(b, 1, MIX_W)
    o_moba = moba_decode(cache_moba, layer, pt_flat, q_m, k_new, v_new, n_pages)
    hqf_t = col(OFF_HQ, 2 * N_HG).T.reshape(2, HG_H, HG_DK, b)
    o_hg, s_out = hgrn_decode(state, layer, hqf_t, col(OFF_HI, N_HG).reshape(b, 1, MIX_W),
                              col(OFF_HGATE, N_HG).reshape(b, 1, MIX_W), lb, lw["hg_norm_w"], state_prev)
    return o_nsa.reshape(b, MIX_W), o_hg.reshape(b, MIX_W), o_moba.reshape(b, MIX_W), win_out, s_out


def prep_layer_weights(i, norm1_w, norm2_w, w_in, nsa_cmp_pos, nsa_cmp_w1, nsa_cmp_w2, hgrn_norm_w,
                       w_branch, w_out, w_up, w_down):
    assert w_in.shape[2] - (N_NQ + N_NKV + N_NGATE) == OFF_NGATE - (N_NQ + N_NKV)
    return dict(layer=i, norm1_w=norm1_w[i], norm2_w=norm2_w[i], w_in_t=jnp.swapaxes(w_in, 1, 2),
                cw=prep_compress_weights(nsa_cmp_pos[i], nsa_cmp_w1[i], nsa_cmp_w2[i]), hg_norm_w=hgrn_norm_w[i],
                w_branch=w_branch, w_out=w_out, w_up=w_up, w_down=w_down)


def kernel(x_prompt, x_sample, cache_nsa_cmp, cache_nsa_slc, cache_moba, cache_nsa_win, state_hgrn,
           page_table, norm1_w, norm2_w, w_in, nsa_cmp_pos, nsa_cmp_w1, nsa_cmp_w2, hgrn_lb_logits,
           hgrn_norm_w, w_branch, w_out, w_up, w_down, final_norm_w):
    sm = jax.nn.softmax(hgrn_lb_logits.astype(jnp.float32), axis=0)
    lbs = jnp.cumsum(sm, axis=0) - sm[0:1]
    xp, xs = x_prompt, x_sample
    outs_p, outs_s = [], []
    stacked = (None, None)
    for i in range(DEPTH):
        lw = prep_layer_weights(i, norm1_w, norm2_w, w_in, nsa_cmp_pos, nsa_cmp_w1, nsa_cmp_w2,
                                hgrn_norm_w, w_branch, w_out, w_up, w_down)
        xp, *st_p = prompt_layer(xp, lbs[i], lw)
        (xs, *st_s), stacked = sample_layer(
            xs, i, (cache_nsa_cmp, cache_nsa_slc, cache_moba, cache_nsa_win, state_hgrn), page_table, lbs[i], lw,
            stacked)
        outs_p.append(st_p)
        outs_s.append(st_s)
    y_prompt = rmsnorm_rows(xp.reshape(-1, D_MODEL), final_norm_w, jnp.float32).reshape(xp.shape)
    y_sample = rmsnorm_rows(xs.reshape(-1, D_MODEL), final_norm_w, jnp.float32).reshape(xs.shape)
    stack = lambda outs, j: jnp.stack([o[j] for o in outs])
    return (y_prompt, y_sample) + tuple(stack(outs_p, j) for j in range(5)) + tuple(
        stack(outs_s, j) for j in range(3)) + stacked
```

```python
import functools

import jax
import jax.numpy as jnp
import numpy as np
from jax import lax
from jax.experimental import pallas as pl
from jax.experimental.pallas import tpu as pltpu

D_MODEL = 2048
DEPTH = 2
PAGE_SIZE = 128
HEAD_DIM = 128
MIX_W = D_MODEL // 2
N_BRANCH = 3
NSA_H = MIX_W // HEAD_DIM
NSA_KVH = NSA_H // 4
CMP_LEN = 32
CMP_STRIDE = 16
SEL_BLOCK = 64
SEL_TOPK = 16
WINDOW = 512
FORCE_BONUS = 1e4
MOBA_H = MIX_W // HEAD_DIM
MOBA_BLOCK = 256
MOBA_TOPK = 3
HG_H = MIX_W // HEAD_DIM
HG_DK = 128
HG_DV = MIX_W // HG_H
EPS = 1e-6
F_FLOOR = 1e-30
NEG = -1e30

N_NQ = NSA_H * HEAD_DIM
N_NKV = 6 * NSA_KVH * HEAD_DIM
N_NGATE = 3 * NSA_H
N_MQKV = 3 * MOBA_H * HEAD_DIM
N_HG = HG_H * HG_DK
N_MG = N_BRANCH * D_MODEL
OFF_NQ = 0
OFF_NKV = OFF_NQ + N_NQ
OFF_MQKV = OFF_NKV + N_NKV
OFF_HQ = OFF_MQKV + N_MQKV
OFF_HF = OFF_HQ + N_HG
OFF_HI = OFF_HF + N_HG
OFF_HGATE = OFF_HI + N_HG
OFF_MG = OFF_HGATE + N_HG
OFF_NGATE = OFF_MG + N_MG
D_IN_PAD = 16384

VMEM_LIMIT_BYTES = 48 * 1024 * 1024


def _cparams(sem):
    return pltpu.CompilerParams(dimension_semantics=sem, vmem_limit_bytes=VMEM_LIMIT_BYTES)


def _rmsnorm_kernel(x_ref, w_ref, o_ref):
    x = x_ref[...]
    y = x * lax.rsqrt(jnp.mean(x * x, axis=-1, keepdims=True) + EPS)
    o_ref[...] = (y * w_ref[...]).astype(o_ref.dtype)


def rmsnorm_rows(x, w, out_dtype):
    m, d = x.shape
    tm = min(m, 512)
    return pl.pallas_call(
        _rmsnorm_kernel,
        grid=(m // tm,),
        in_specs=[pl.BlockSpec((tm, d), lambda i: (i, 0)), pl.BlockSpec((1, d), lambda i: (0, 0))],
        out_specs=pl.BlockSpec((tm, d), lambda i: (i, 0)),
        out_shape=jax.ShapeDtypeStruct((m, d), out_dtype),
        compiler_params=_cparams(("parallel",)),
        name="rmsnorm",
    )(x, w.reshape(1, d))


def _mm_kernel(*refs, epilogue, nk):
    if epilogue == "residual":
        a_ref, w_ref, r_ref, o_ref = refs[:4]
        rest = refs[4:]
    else:
        a_ref, w_ref, o_ref = refs[:3]
        r_ref = None
        rest = refs[3:]
    part = jnp.dot(a_ref[...], w_ref[...].astype(jnp.bfloat16), preferred_element_type=jnp.float32)

    def finish(acc):
        if epilogue == "residual":
            o_ref[...] = r_ref[...] + acc
        elif epilogue == "relu2":
            u = jnp.maximum(acc, 0.0)
            o_ref[...] = (u * u).astype(o_ref.dtype)
        else:
            o_ref[...] = acc.astype(o_ref.dtype)

    if nk == 1:
        finish(part)
    else:
        acc_ref = rest[0]
        k = pl.program_id(2)

        @pl.when(k == 0)
        def _():
            acc_ref[...] = part

        @pl.when(k > 0)
        def _():
            acc_ref[...] += part

        @pl.when(k == nk - 1)
        def _():
            finish(acc_ref[...])


def matmul(a, w, layer, *, epilogue="none", residual=None, out_dtype=jnp.float32, tm=1024, tn=1024, tk=2048):
    m, kdim = a.shape
    n = w.shape[2]
    tm, tn, tk = min(tm, m), min(tn, n), min(tk, kdim)
    nk = kdim // tk
    in_specs = [pl.BlockSpec((tm, tk), lambda j, i, k: (i, k)),
                pl.BlockSpec((None, tk, tn), lambda j, i, k: (layer, k, j))]
    args = [a, w]
    if epilogue == "residual":
        in_specs.append(pl.BlockSpec((tm, tn), lambda j, i, k: (i, j)))
        args.append(residual)
    scratch = [pltpu.VMEM((tm, tn), jnp.float32)] if nk > 1 else []
    return pl.pallas_call(
        functools.partial(_mm_kernel, epilogue=epilogue, nk=nk),
        grid=(n // tn, m // tm, nk),
        in_specs=in_specs,
        out_specs=pl.BlockSpec((tm, tn), lambda j, i, k: (i, j)),
        out_shape=jax.ShapeDtypeStruct((m, n), out_dtype),
        scratch_shapes=scratch,
        compiler_params=_cparams(("parallel", "parallel", "arbitrary")),
        name="matmul_" + epilogue,
    )(*args)


W_IN_TN = 512


def _w_in_row(j):
    n_lo = (N_NQ + N_NKV) // W_IN_TN
    gate_tile = OFF_NGATE // W_IN_TN
    row = jnp.where(j < n_lo, j * W_IN_TN,
                    jnp.where(j < gate_tile, N_NQ + N_NKV + N_NGATE + (j - n_lo) * W_IN_TN, N_NQ + N_NKV))
    return pl.multiple_of(row, 8)


def _project_kernel(a_ref, wt_ref, o_ref):
    o_ref[...] = lax.dot_general(a_ref[...], wt_ref[0].astype(jnp.bfloat16), (((1,), (1,)), ((), ())),
                                 preferred_element_type=jnp.float32)


def project_in(a, w_in_t, layer, *, tm=2048):
    m, d = a.shape
    assert (N_NQ + N_NKV) % W_IN_TN == 0 and OFF_NGATE % W_IN_TN == 0 and D_IN_PAD == OFF_NGATE + W_IN_TN
    tm = min(tm, m)
    return pl.pallas_call(
        _project_kernel,
        grid=(m // tm, D_IN_PAD // W_IN_TN),
        in_specs=[pl.BlockSpec((tm, d), lambda i, j: (i, 0)),
                  pl.BlockSpec((pl.Element(1), pl.Element(W_IN_TN), pl.Element(d)),
                               lambda i, j: (layer, _w_in_row(j), 0))],
        out_specs=pl.BlockSpec((tm, W_IN_TN), lambda i, j: (i, j)),
        out_shape=jax.ShapeDtypeStruct((m, D_IN_PAD), jnp.float32),
        compiler_params=_cparams(("parallel", "parallel")),
        name="project_in",
    )(a, w_in_t)


def _merge_kernel(b0_ref, b1_ref, b2_ref, wb_ref, g0_ref, g1_ref, g2_ref, o_ref):
    acc = None
    for n, (b_ref, g_ref) in enumerate(((b0_ref, g0_ref), (b1_ref, g1_ref), (b2_ref, g2_ref))):
        proj = jnp.dot(b_ref[...], wb_ref[n].astype(jnp.bfloat16), preferred_element_type=jnp.float32)
        term = jax.nn.sigmoid(g_ref[...]) * proj
        acc = term if acc is None else acc + term
    o_ref[...] = acc.astype(o_ref.dtype)


def branch_merge(branches, wb, layer, z, *, tm=512, tn=512):
    m = branches[0].shape[0]
    tm = min(tm, m)
    gate_specs = [
        pl.BlockSpec((tm, tn), lambda j, i, n=n: (i, (OFF_MG + n * D_MODEL) // tn + j)) for n in range(N_BRANCH)]
    return pl.pallas_call(
        _merge_kernel,
        grid=(D_MODEL // tn, m // tm),
        in_specs=[pl.BlockSpec((tm, MIX_W), lambda j, i: (i, 0))] * N_BRANCH
        + [pl.BlockSpec((None, N_BRANCH, MIX_W, tn), lambda j, i: (layer, 0, 0, j))] + gate_specs,
        out_specs=pl.BlockSpec((tm, tn), lambda j, i: (i, j)),
        out_shape=jax.ShapeDtypeStruct((m, D_MODEL), jnp.bfloat16),
        compiler_params=_cparams(("parallel", "parallel")),
        name="branch_merge",
    )(*branches, wb, z, z, z)


SCALE = HEAD_DIM ** -0.5
NSA_REP = NSA_H // NSA_KVH
CMP_PER_PAGE = PAGE_SIZE // CMP_STRIDE


def _bf(x):
    return x.astype(jnp.bfloat16)


def _split3(x):
    hi = _bf(x)
    r1 = x - hi.astype(jnp.float32)
    mid = _bf(r1)
    lo = _bf(r1 - mid.astype(jnp.float32))
    return hi, mid, lo


_NT = (((1,), (1,)), ((), ()))


def _dot(a, b):
    return jnp.dot(a, b, preferred_element_type=jnp.float32)


def _dot_nt(a, b):
    return lax.dot_general(a, b, _NT, preferred_element_type=jnp.float32)


def _dot_f32_lhs(a, b_exact):
    return sum(_dot(p, b_exact) for p in _split3(a))


def _dot_f32_rhs(a_exact, b):
    return sum(_dot(a_exact, p) for p in _split3(b))


def _dot_nt_f32(a, b):
    a1, a2, a3 = _split3(a)
    b1, b2, b3 = _split3(b)
    return (_dot_nt(a1, b1) + (_dot_nt(a1, b2) + _dot_nt(a2, b1))
            + (_dot_nt(a1, b3) + _dot_nt(a2, b2) + _dot_nt(a3, b1)))


def _masked_softmax(s, mask):
    s = jnp.where(mask, s, NEG)
    m = jnp.max(s, axis=-1, keepdims=True)
    e = jnp.where(mask, jnp.exp(s - m), 0.0)
    l = jnp.sum(e, axis=-1, keepdims=True)
    return e / jnp.where(l > 0.0, l, 1.0)


def _masked_softmax_rows(s, mask):
    s = jnp.where(mask, s, NEG)
    m = jnp.max(s, axis=0, keepdims=True)
    e = jnp.where(mask, jnp.exp(s - m), 0.0)
    l = jnp.sum(e, axis=0, keepdims=True)
    return e / jnp.where(l > 0.0, l, 1.0)


def _rank_lanes(score, n):
    lane = lax.broadcasted_iota(jnp.int32, score.shape, 1)
    rank = jnp.zeros(score.shape, jnp.float32)
    for i in range(n):
        col = score[:, i:i + 1]
        ahead = jnp.where(col == score, jnp.where(lane > i, 1.0, 0.0), jnp.where(col > score, 1.0, 0.0))
        rank = rank + ahead
    return rank


def _rank_sublanes(score, n):
    row = lax.broadcasted_iota(jnp.int32, score.shape, 0)
    rank = jnp.zeros(score.shape, jnp.float32)
    for i in range(n):
        r = score[i:i + 1, :]
        ahead = jnp.where(r == score, jnp.where(row > i, 1.0, 0.0), jnp.where(r > score, 1.0, 0.0))
        rank = rank + ahead
    return rank


def _compress_tokens(x_bf, w1ab_ref, w2_ref, posrows_ref):
    n = x_bf.shape[0]
    w1ab = w1ab_ref[...]
    pre = _dot(x_bf, w1ab)
    pb = _dot(posrows_ref[...], w1ab)
    posbias = pb[0:1, :HEAD_DIM] + pb[1:2, HEAD_DIM:]
    nxt = pltpu.roll(pre[:, HEAD_DIM:], n - 1, 0)
    hid = pre[:, :HEAD_DIM] + nxt + posbias
    return _dot(_bf(jax.nn.silu(hid)), w2_ref[...])


def prep_compress_weights(cmp_pos, cmp_w1, cmp_w2):
    half = CMP_STRIDE * HEAD_DIM
    w1ab = jnp.concatenate([cmp_w1[:, :half], cmp_w1[:, half:]], axis=-1)
    pos2 = cmp_pos.reshape(2, 2, half)
    posrows = jnp.concatenate([pos2, jnp.zeros((2, 6, half), cmp_pos.dtype)], axis=1)
    return _bf(w1ab), _bf(cmp_w2), _bf(posrows)


def _cmp_prompt_kernel(x_ref, w1ab_ref, w2_ref, posrows_ref, o_ref):
    n_chunks = x_ref.shape[0] // CMP_STRIDE
    xc = jnp.concatenate(
        [_bf(x_ref[pl.ds(j, n_chunks, stride=CMP_STRIDE), :]) for j in range(CMP_STRIDE)], axis=-1)
    o_ref[...] = _compress_tokens(xc, w1ab_ref, w2_ref, posrows_ref)


def nsa_compress_prompt(z, b, t, cw):
    w1ab, w2, posrows = cw
    n_chunks = t // CMP_STRIDE
    kind_spec = lambda a: pl.BlockSpec((None,) + a.shape[1:], lambda i, c: (c // NSA_KVH, 0, 0))
    return pl.pallas_call(
        _cmp_prompt_kernel,
        grid=(b, 2 * NSA_KVH),
        in_specs=[pl.BlockSpec((t, HEAD_DIM), lambda i, c: (i, OFF_NKV // HEAD_DIM + c)),
                  kind_spec(w1ab), kind_spec(w2), kind_spec(posrows)],
        out_specs=pl.BlockSpec((None, None, n_chunks, HEAD_DIM), lambda i, c: (i, c, 0, 0)),
        out_shape=jax.ShapeDtypeStruct((b, 2 * NSA_KVH, n_chunks, HEAD_DIM), jnp.float32),
        compiler_params=_cparams(("parallel", "parallel")),
        name="nsa_compress_prompt",
    )(z, w1ab, w2, posrows)


NSA_TQ = 128
NSA_KEY_SPANS = 4


def _nsa_prompt_kernel(q_ref, ckv_ref, ks_ref, vs_ref, kw_ref, vw_ref, gate_ref, ovl_ref, exp_ref, o_ref,
                       ks_bf, vs_bf, kw_bf, vw_bf, oslc_sc, *, t):
    qt = pl.program_id(1)
    q0 = pl.multiple_of(qt * NSA_TQ, NSA_TQ)
    n_cmp = ckv_ref.shape[1]
    n_sel = t // SEL_BLOCK
    span = WINDOW + NSA_TQ
    nqt = t // NSA_TQ
    n_span = min(NSA_KEY_SPANS, nqt)

    @pl.when(qt == 0)
    def _():
        ks_bf[...] = _bf(ks_ref[...])
        vs_bf[...] = _bf(vs_ref[...])
        kw_bf[...] = _bf(kw_ref[...])
        vw_bf[...] = _bf(vw_ref[...])

    qpos = q0 + lax.broadcasted_iota(jnp.int32, (NSA_TQ, 1), 0)
    gates = jax.nn.sigmoid(gate_ref[...])
    lane = lax.broadcasted_iota(jnp.int32, (NSA_TQ, 128), 1)
    lane_c = lax.broadcasted_iota(jnp.int32, (NSA_TQ, n_cmp), 1)
    cmask = (lane_c * CMP_STRIDE + (CMP_LEN - 1) <= qpos) & (lane_c < n_cmp - 1)
    cur = qpos // SEL_BLOCK
    forced = (lane == 0) | (lane == cur) | (lane == cur - 1)
    wstart =pl.multiple_of(jnp.maximum(q0 - WINDOW, 0), NSA_TQ)
    wd = qpos - (wstart + lax.broadcasted_iota(jnp.int32, (NSA_TQ, span), 1))
    wmask = (wd >= 0) & (wd <= WINDOW)

    def attend(qs, k_bf, v_bf, mask):
        nk = k_bf.shape[0]
        s = _dot_nt(qs, k_bf).reshape(NSA_REP, NSA_TQ, nk) * SCALE
        p = _masked_softmax(s, mask[None])
        return p, _dot(_bf(p.reshape(NSA_REP * NSA_TQ, nk)), v_bf)

    for g in range(NSA_KVH):
        qs = _bf(jnp.concatenate(
            [q_ref[:, (g * NSA_REP + r) * HEAD_DIM:(g * NSA_REP + r + 1) * HEAD_DIM] for r in range(NSA_REP)],
            axis=0))
        p_cmp, o_cmp = attend(qs, _bf(ckv_ref[g]), _bf(ckv_ref[NSA_KVH + g]), cmask)
        psum = p_cmp[0] + p_cmp[1] + p_cmp[2] + p_cmp[3]
        imp = _dot_f32_lhs(psum, ovl_ref[...])
        score = jnp.where(lane <= cur, imp + jnp.where(forced, FORCE_BONUS, 0.0), NEG)
        rank = _rank_lanes(score, n_sel)
        n_ok = jnp.minimum(cur + 1, min(SEL_TOPK, n_sel)).astype(jnp.float32)
        sel = jnp.where(rank < n_ok, 1.0, 0.0)
        sel_b = _bf(sel)
        gsl = slice(g * HEAD_DIM, (g + 1) * HEAD_DIM)

        def selected(nk, sel_b=sel_b, qs=qs, gsl=gsl):
            keysel = _dot(sel_b, exp_ref[:, 0:nk])
            kpos = lax.broadcasted_iota(jnp.int32, (NSA_TQ, nk), 1)
            smask = jnp.where(kpos <= qpos, keysel, 0.0) > 0.5
            oslc_sc[...] = attend(qs, ks_bf[0:nk, gsl], vs_bf[0:nk, gsl], smask)[1]

        for j in range(n_span):
            pl.when(qt // (nqt // n_span) == j)(functools.partial(selected, (j + 1) * (t // n_span)))
        o_slc = oslc_sc[...]
        _, o_win = attend(qs, kw_bf[pl.ds(wstart, span), gsl], vw_bf[pl.ds(wstart, span), gsl], wmask)
        for r in range(NSA_REP):
            h = g * NSA_REP + r
            rows = slice(r * NSA_TQ, (r + 1) * NSA_TQ)
            o = (gates[:, h:h + 1] * o_cmp[rows] + gates[:, NSA_H + h:NSA_H + h + 1] * o_slc[rows]
                 + gates[:, 2 * NSA_H + h:2 * NSA_H + h + 1] * o_win[rows])
            o_ref[:, h * HEAD_DIM:(h + 1) * HEAD_DIM] = o.astype(o_ref.dtype)


def nsa_prompt(z, ckv, b, t):
    n_cmp = ckv.shape[2]
    n_sel = t // SEL_BLOCK
    nqt = t // NSA_TQ
    gw = NSA_KVH * HEAD_DIM
    cmp_start = np.arange(n_cmp) * CMP_STRIDE
    sel_start = np.arange(128) * SEL_BLOCK
    ovl = ((cmp_start[:, None] < sel_start[None, :] + SEL_BLOCK) & (cmp_start[:, None] + CMP_LEN > sel_start[None, :])
           & (np.arange(n_cmp)[:, None] < n_cmp - 1) & (np.arange(128)[None, :] < n_sel))
    ovl = jnp.asarray(ovl, jnp.bfloat16)
    expand = jnp.asarray(np.arange(128)[:, None] == (np.arange(t)[None, :] // SEL_BLOCK), jnp.bfloat16)
    kv_spec = lambda kind: pl.BlockSpec((t, gw), lambda i, j: (i, OFF_NKV // gw + kind))
    return pl.pallas_call(
        functools.partial(_nsa_prompt_kernel, t=t),
        grid=(b, nqt),
        in_specs=[pl.BlockSpec((NSA_TQ, N_NQ), lambda i, j: (i * nqt + j, OFF_NQ // N_NQ)),
                  pl.BlockSpec((None, 2 * NSA_KVH, n_cmp, HEAD_DIM), lambda i, j: (i, 0, 0, 0)),
                  kv_spec(2), kv_spec(3), kv_spec(4), kv_spec(5),
                  pl.BlockSpec((NSA_TQ, 128), lambda i, j: (i * nqt + j, OFF_NGATE // 128)),
                  pl.BlockSpec(ovl.shape, lambda i, j: (0, 0)),
                  pl.BlockSpec(expand.shape, lambda i, j: (0, 0))],
        out_specs=pl.BlockSpec((NSA_TQ, MIX_W), lambda i, j: (i * nqt + j, 0)),
        out_shape=jax.ShapeDtypeStruct((b * t, MIX_W), jnp.bfloat16),
        scratch_shapes=[pltpu.VMEM((t, gw), jnp.bfloat16)] * 4
        + [pltpu.VMEM((NSA_REP * NSA_TQ, HEAD_DIM), jnp.float32)],
        compiler_params=_cparams(("parallel", "arbitrary")),
        name="nsa_prompt",
    )(z, ckv, z, z, z, z, z, ovl, expand)


def _moba_prompt_kernel(q_ref, k_ref, v_ref, exp_ref, o_ref, k_bf, v_bf, km_ref, *, t):
    qt = pl.program_id(2)
    nb = t // MOBA_BLOCK

    @pl.when(qt == 0)
    def _():
        k = k_ref[...]
        k_bf[...] = _bf(k)
        v_bf[...] = _bf(v_ref[...])
        km_ref[...] = jnp.zeros(km_ref.shape, jnp.float32)
        km_ref[0:nb, :] = jnp.mean(k.reshape(nb, MOBA_BLOCK, HEAD_DIM), axis=1)

    q = q_ref[...]
    qpos = qt * MOBA_BLOCK + lax.broadcasted_iota(jnp.int32, (MOBA_BLOCK, 1), 0)
    lane = lax.broadcasted_iota(jnp.int32, (MOBA_BLOCK, 128), 1)
    gate = jnp.where(lane < qt, _dot_nt_f32(q, km_ref[...]), NEG)
    rank = _rank_lanes(gate, nb)
    n_ok = jnp.minimum(qt, min(MOBA_TOPK, nb)).astype(jnp.float32)
    sel = _bf(jnp.where(rank < n_ok, 1.0, 0.0))
    q_bf = _bf(q)

    def attend(nk):
        keysel = _dot(sel, exp_ref[:, 0:nk])
        kpos = lax.broadcasted_iota(jnp.int32, (MOBA_BLOCK, nk), 1)
        own = (kpos // MOBA_BLOCK == qt) & (kpos <= qpos)
        mask = jnp.where(own, 1.0, keysel) > 0.5
        s = _dot_nt(q_bf, k_bf[0:nk, :]) * SCALE
        p = _masked_softmax(s, mask)
        o_ref[...] = _dot(_bf(p), v_bf[0:nk, :]).astype(o_ref.dtype)

    for j in range(nb):
        pl.when(qt == j)(functools.partial(attend, (j + 1) * MOBA_BLOCK))


def moba_prompt(z, b, t):
    nqt = t // MOBA_BLOCK
    expand = jnp.asarray(np.arange(128)[:, None] == (np.arange(t)[None, :] // MOBA_BLOCK), jnp.bfloat16)
    col = lambda part: OFF_MQKV // HEAD_DIM + part * MOBA_H
    return pl.pallas_call(
        functools.partial(_moba_prompt_kernel, t=t),
        grid=(b, MOBA_H, nqt),
        in_specs=[pl.BlockSpec((MOBA_BLOCK, HEAD_DIM), lambda i, h, j: (i * nqt + j, col(0) + h)),
                  pl.BlockSpec((t, HEAD_DIM), lambda i, h, j: (i, col(1) + h)),
                  pl.BlockSpec((t, HEAD_DIM), lambda i, h, j: (i, col(2) + h)),
                  pl.BlockSpec(expand.shape, lambda i, h, j: (0, 0))],
        out_specs=pl.BlockSpec((MOBA_BLOCK, HEAD_DIM), lambda i, h, j: (i * nqt + j, h)),
        out_shape=jax.ShapeDtypeStruct((b * t, MIX_W), jnp.bfloat16),
        scratch_shapes=[pltpu.VMEM((t, HEAD_DIM), jnp.bfloat16), pltpu.VMEM((t, HEAD_DIM), jnp.bfloat16),
                        pltpu.VMEM((128, HEAD_DIM), jnp.float32)],
        compiler_params=_cparams(("parallel", "parallel", "arbitrary")),
        name="moba_prompt",
    )(z, z, z, expand)


HG_KCHUNK = 128
HG_SUB = 16
HG_HEADS_PER_STEP = 4


def _hgrn_gates(hq, hf, lb):
    q = jax.nn.silu(hq)
    f = lb + (1.0 - lb) * jax.nn.sigmoid(hf)
    logf = jnp.log(jnp.maximum(f, F_FLOOR))
    k = (1.0 - lb) * jax.nn.sigmoid(-hf)
    return q, k, logf


def _hgrn_out(o, hg, norm_w):
    o = o * lax.rsqrt(jnp.mean(o * o, axis=-1, keepdims=True) + EPS)
    return o * norm_w * jax.nn.silu(hg)


def _hgrn_prompt_kernel(hq_ref, hf_ref, hi_ref, hg_ref, lb_ref, nw_ref, tri_ref, o_ref, s_ref,
                        st_ref, kcv_sc, *, t):
    c = HG_KCHUNK
    st_ref[...] = jnp.zeros(st_ref.shape, jnp.float32)
    row_s = lax.broadcasted_iota(jnp.int32, (HG_SUB, 1), 0)
    col_c = lax.broadcasted_iota(jnp.int32, (HG_SUB, c), 1)

    def head_chunk(hh, rows):
        lanes = slice(hh * HG_DK, (hh + 1) * HG_DK)
        lb = lb_ref[:, lanes]
        nw = nw_ref[:, lanes]
        k_sc, cum_sc, v_sc = kcv_sc.at[hh, 0], kcv_sc.at[hh, 1], kcv_sc.at[hh, 2]
        q, k, logf = _hgrn_gates(hq_ref[rows, lanes], hf_ref[rows, lanes], lb)
        v = hi_ref[rows, lanes]
        cum = _dot_f32_rhs(tri_ref[...], logf)
        k_sc[...] = k
        cum_sc[...] = cum
        v_sc[...] = v
        v_b = _bf(v)
        st = st_ref[hh]
        o_carry = _dot_nt(_bf(q * jnp.exp(cum)), _bf(st))
        o_parts = []
        for i in range(c // HG_SUB):
            r0 = i * HG_SUB
            q_i = q[r0:r0 + HG_SUB]
            cum_i = cum[r0:r0 + HG_SUB]
            o_i = o_carry[r0:r0 + HG_SUB]
            if i > 0:
                edge = cum[r0 - 1:r0]
                k_e = k * jnp.exp(jnp.minimum(edge - cum, 0.0))
                a = _dot_nt(_bf(q_i * jnp.exp(cum_i - edge)), _bf(k_e))
                o_i = o_i + _dot(_bf(jnp.where(col_c < r0, a, 0.0)), v_b)
            for s in range(r0, r0 + HG_SUB):
                d = jnp.minimum(cum_i - cum_sc[pl.ds(s, 1), :], 0.0)
                w = jnp.sum(q_i * jnp.exp(d) * k_sc[pl.ds(s, 1), :], axis=-1, keepdims=True)
                o_i = o_i + jnp.where(row_s >= s - r0, w, 0.0) * v_sc[pl.ds(s, 1), :]
            o_parts.append(o_i)
        o = jnp.concatenate(o_parts, axis=0)
        o_ref[rows, lanes] = _hgrn_out(o, hg_ref[rows, lanes], nw).astype(o_ref.dtype)
        last = cum[c - 1:c]
        kd = k * jnp.exp(last - cum)
        upd = lax.dot_general(v_b, _bf(kd), (((0,), (0,)), ((), ())), preferred_element_type=jnp.float32)
        st_ref[hh] = st * jnp.exp(last) + upd

    def chunk(ci, carry):
        rows = pl.ds(pl.multiple_of(ci * c, c), c)
        for hh in range(HG_HEADS_PER_STEP):
            head_chunk(hh, rows)
        return carry

    lax.fori_loop(0, t // c, chunk, 0)
    for hh in range(HG_HEADS_PER_STEP):
        s_ref[hh] = st_ref[hh].T


def hgrn_prompt(z, lb, norm_w, b, t):
    tri = jnp.asarray(np.tril(np.ones((HG_KCHUNK, HG_KCHUNK))), jnp.bfloat16)
    hp = HG_HEADS_PER_STEP
    width = hp * HG_DK
    col = lambda off: (lambda i, h: (i, off // width + h))
    vec = pl.BlockSpec((1, width), lambda i, h: (0, h))
    return pl.pallas_call(
        functools.partial(_hgrn_prompt_kernel, t=t),
        grid=(b, HG_H // hp),
        in_specs=[pl.BlockSpec((t, width), col(OFF_HQ)), pl.BlockSpec((t, width), col(OFF_HF)),
                  pl.BlockSpec((t, width), col(OFF_HI)), pl.BlockSpec((t, width), col(OFF_HGATE)),
                  vec, vec, pl.BlockSpec(tri.shape, lambda i, h: (0, 0))],
        out_specs=[pl.BlockSpec((t, width), lambda i, h: (i, h)),
                   pl.BlockSpec((None, hp, HG_DK, HG_DV), lambda i, h: (i, h, 0, 0))],
        out_shape=[jax.ShapeDtypeStruct((b * t, MIX_W), jnp.bfloat16),
                   jax.ShapeDtypeStruct((b, HG_H, HG_DK, HG_DV), jnp.float32)],
        scratch_shapes=[pltpu.VMEM((hp, HG_DV, HG_DK), jnp.float32),
                        pltpu.VMEM((hp, 3, HG_KCHUNK, HG_DK), jnp.float32)],
        compiler_params=_cparams(("parallel", "parallel")),
        name="hgrn_prompt",
    )(z, z, z, z, lb.reshape(1, MIX_W), norm_w.reshape(1, MIX_W), tri)


SEL_ROWS = 40


def _qbd(q, heads_per_kv):
    n_kv = q.shape[0] // heads_per_kv
    row = lax.broadcasted_iota(jnp.int32, q.shape, 0)
    blocks = [jnp.where(row // heads_per_kv == g, q, 0.0) for g in range(n_kv)]
    top = jnp.concatenate(blocks, axis=1)
    return jnp.concatenate([top, jnp.zeros((128 - q.shape[0], top.shape[1]), jnp.float32)], axis=0)


def _softmax_keys(s_sc, s_new, new_ok):
    s = s_sc[...]
    s_new = jnp.where(new_ok, s_new, NEG)
    m = jnp.maximum(jnp.max(s, axis=0, keepdims=True), s_new)
    e = jnp.where(s > 0.5 * NEG, jnp.exp(s - m), 0.0)
    e_new = jnp.where(new_ok, jnp.exp(s_new - m), 0.0)
    l = jnp.sum(e, axis=0, keepdims=True) + e_new
    inv = 1.0 / jnp.where(l > 0.0, l, 1.0)
    return e * inv, e_new * inv


def _cmp_decode_kernel(pt_ref, *refs, n_pages, pos):
    pages = refs[:n_pages]
    q_ref, w1ab_ref, w2_ref, posrows_ref, ovlt_ref, grp_ref, o_ref, sel_ref, xs = refs[n_pages:]
    n_tok = n_pages * CMP_PER_PAGE
    for pp in range(n_pages // 2):
        for j in range(CMP_STRIDE):
            for c in range(2 * NSA_KVH):
                piece = jnp.concatenate(
                    [_stream(pages[2 * pp + i], c, j, CMP_PER_PAGE, CMP_STRIDE) for i in range(2)], axis=0)
                xs[c, 2 * CMP_PER_PAGE * pp:2 * CMP_PER_PAGE * (pp + 1), j * HEAD_DIM:(j + 1) * HEAD_DIM] = _bf(piece)
    kc = [_compress_tokens(xs[g], w1ab_ref.at[0], w2_ref.at[0], posrows_ref.at[0]) for g in range(NSA_KVH)]
    vc = [_compress_tokens(xs[NSA_KVH + g], w1ab_ref.at[1], w2_ref.at[1], posrows_ref.at[1])
          for g in range(NSA_KVH)]
    qbd = _bf(_qbd(q_ref[...], NSA_REP))
    s = _dot_nt(_bf(jnp.concatenate(kc, axis=1)), qbd) * SCALE
    tok = lax.broadcasted_iota(jnp.int32, s.shape, 0)
    p = _masked_softmax_rows(s, (tok * CMP_STRIDE + (CMP_LEN - 1) <= pos) & (tok < n_tok - 1))
    o_ref[...] = jnp.concatenate(
        [jnp.sum(p[:, h:h + 1] * vc[h // NSA_REP], axis=0, keepdims=True) for h in range(NSA_H)], axis=0)
    imp = _dot_f32_rhs(ovlt_ref[...], _dot_f32_lhs(p, grp_ref[...]))
    blk = lax.broadcasted_iota(jnp.int32, imp.shape, 0)
    cur = pos // SEL_BLOCK
    forced = (blk == 0) | (blk == cur) | (blk == cur - 1)
    score = jnp.where(blk <= cur, imp + jnp.where(forced, FORCE_BONUS, 0.0), NEG)
    n_sel = cur + 1
    rank = _rank_sublanes(score, n_sel)
    sel_ref[...] = jnp.where(rank < float(min(SEL_TOPK, n_sel)), 1.0, 0.0)


def _page_specs(n_pages, layer, heads, phased=False):
    def spec(p):
        if phased:
            return pl.BlockSpec((None, None, PAGE_SIZE, None, heads, HEAD_DIM),
                                lambda b, ph, pt: (layer, pt[b * n_pages + p], 0, ph, 0, 0))
        return pl.BlockSpec((None, None, PAGE_SIZE, 2, heads, HEAD_DIM),
                            lambda b, pt: (layer, pt[b * n_pages + p], 0, 0, 0, 0))
    return [spec(p) for p in range(n_pages)]


def _stream(ref, idx, first=0, count=None, step=1):
    rows = ref.shape[0]
    streams = int(np.prod(ref.shape[1:-1]))
    count = rows if count is None else count
    flat = ref.reshape(rows * streams, ref.shape[-1])
    return flat[pl.ds(first * streams + idx, count, stride=step * streams), :]


def _heads_on_lanes(ref, kind, heads):
    return jnp.concatenate([_stream(ref, kind * heads + h) for h in range(heads)], axis=1)


def nsa_cmp_decode(cache_cmp, layer, pt_flat, q_s, cw, n_pages):
    w1ab, w2, posrows = cw
    b = q_s.shape[0]
    pos = n_pages * PAGE_SIZE
    n_tok = n_pages * CMP_PER_PAGE
    cmp_start = np.arange(n_tok) * CMP_STRIDE
    sel_start = np.arange(SEL_ROWS) * SEL_BLOCK
    ovlt = ((cmp_start[None, :] < sel_start[:, None] + SEL_BLOCK) & (cmp_start[None, :] + CMP_LEN > sel_start[:, None])
            & (np.arange(n_tok)[None, :] < n_tok - 1))
    hh = np.arange(128)
    grp = (hh[:, None] // NSA_REP == hh[None, :] // NSA_REP) & (hh[:, None] < NSA_H) & (hh[None, :] < NSA_H)
    ovlt, grp = jnp.asarray(ovlt, jnp.bfloat16), jnp.asarray(grp, jnp.bfloat16)
    const = lambda a: pl.BlockSpec(a.shape, lambda i, pt: (0,) * a.ndim)
    grid_spec = pltpu.PrefetchScalarGridSpec(
        num_scalar_prefetch=1, grid=(b,),
        in_specs=_page_specs(n_pages, layer, NSA_KVH)
        + [pl.BlockSpec((None, NSA_H, HEAD_DIM), lambda i, pt: (i, 0, 0)),
           const(w1ab), const(w2), const(posrows), const(ovlt), const(grp)],
        out_specs=[pl.BlockSpec((None, NSA_H, HEAD_DIM), lambda i, pt: (i, 0, 0)),
                   pl.BlockSpec((None, SEL_ROWS, 128), lambda i, pt: (i, 0, 0))],
        scratch_shapes=[pltpu.VMEM((2 * NSA_KVH, n_tok, CMP_STRIDE * HEAD_DIM), jnp.bfloat16)])
    return pl.pallas_call(
        functools.partial(_cmp_decode_kernel, n_pages=n_pages, pos=pos),
        grid_spec=grid_spec,
        out_shape=[jax.ShapeDtypeStruct((b, NSA_H, HEAD_DIM), jnp.float32),
                   jax.ShapeDtypeStruct((b, SEL_ROWS, 128), jnp.float32)],
        compiler_params=_cparams(("parallel",)),
        name="nsa_cmp_decode",
    )(pt_flat, *([cache_cmp] * n_pages), q_s, w1ab, w2, posrows, ovlt, grp)


def _slc_decode_kernel(pt_ref, *refs, n_pages, pos):
    pages = refs[:n_pages]
    q_ref, kv_ref, sel_ref, o_ref, s_sc = refs[n_pages:]
    kw = NSA_KVH * HEAD_DIM
    qbd = _bf(_qbd(q_ref[...], NSA_REP))
    row = lax.broadcasted_iota(jnp.int32, (PAGE_SIZE, 128), 0)
    per_page = PAGE_SIZE // SEL_BLOCK
    for p in range(n_pages):
        s = _dot_nt(_bf(_heads_on_lanes(pages[p], 0, NSA_KVH)), qbd) * SCALE
        sel = sel_ref[pl.ds(per_page * p, 1), :]
        for i in range(1, per_page):
            sel = jnp.where(row >= i * SEL_BLOCK, sel_ref[pl.ds(per_page * p + i, 1), :], sel)
        kpos = p * PAGE_SIZE + row
        s_sc[p * PAGE_SIZE:(p + 1) * PAGE_SIZE, :] = jnp.where(kpos <= pos, jnp.where(sel > 0.5, s, NEG), NEG)
    k_new = kv_ref[:, 2 * kw:3 * kw]
    v_new = kv_ref[:, 3 * kw:4 * kw]
    s_new = _dot_nt(_bf(jnp.broadcast_to(k_new, (8, kw))), qbd)[0:1] * SCALE
    new_ok = sel_ref[pl.ds(pos // SEL_BLOCK, 1), :] > 0.5
    _softmax_keys_into(s_sc, s_new, new_ok, o_ref, pages, v_new, NSA_REP, PAGE_SIZE)


def _softmax_keys_into(s_sc, s_new, new_ok, o_ref, pages, v_new, heads_per_kv, rows):
    p_all, p_new = _softmax_keys(s_sc, s_new, new_ok)
    s_sc[...] = p_all
    n_heads = o_ref.shape[0]
    n_kv = n_heads // heads_per_kv
    acc = [jnp.zeros((n_heads, HEAD_DIM), jnp.float32)] * n_kv
    for p, page in enumerate(pages):
        p_bf = _bf(s_sc[p * rows:(p + 1) * rows, :])
        for g in range(n_kv):
            pv = lax.dot_general(p_bf, _bf(_stream(page, n_kv + g)), (((0,), (0,)), ((), ())),
                                 preferred_element_type=jnp.float32)
            acc[g] = acc[g] + pv[0:n_heads]
    head = lax.broadcasted_iota(jnp.int32, (n_heads, 1), 0)
    o = jnp.zeros((n_heads, HEAD_DIM), jnp.float32)
    for g in range(n_kv):
        new = jnp.concatenate([p_new[:, h:h + 1] for h in range(n_heads)], axis=0) * v_new[:, g * HEAD_DIM:(g + 1) * HEAD_DIM]
        o = o + jnp.where(head // heads_per_kv == g, acc[g] + new, 0.0)
    o_ref[...] = o.astype(o_ref.dtype)


def nsa_slc_decode(cache_slc, layer, pt_flat, q_s, kv_new, sel, n_pages):
    b = q_s.shape[0]
    pos = n_pages * PAGE_SIZE
    grid_spec = pltpu.PrefetchScalarGridSpec(
        num_scalar_prefetch=1, grid=(b,),
        in_specs=_page_specs(n_pages, layer, NSA_KVH)
        + [pl.BlockSpec((None, NSA_H, HEAD_DIM), lambda i, pt: (i, 0, 0)),
           pl.BlockSpec((None, 1, N_NKV), lambda i, pt: (i, 0, 0)),
           pl.BlockSpec((None, SEL_ROWS, 128), lambda i, pt: (i, 0, 0))],
        out_specs=pl.BlockSpec((None, NSA_H, HEAD_DIM), lambda i, pt: (i, 0, 0)),
        scratch_shapes=[pltpu.VMEM((n_pages * PAGE_SIZE, 128), jnp.float32)])
    return pl.pallas_call(
        functools.partial(_slc_decode_kernel, n_pages=n_pages, pos=pos),
        grid_spec=grid_spec,
        out_shape=jax.ShapeDtypeStruct((b, NSA_H, HEAD_DIM), jnp.float32),
        compiler_params=_cparams(("parallel",)),
        name="nsa_slc_decode",
    )(pt_flat, *([cache_slc] * n_pages), q_s, kv_new, sel)


def _win_decode_kernel(win_ref, q_ref, kv_ref, ocmp_ref, oslc_ref, gate_ref, *rest, pos, aliased):
    o_ref, wout_ref, s_sc, ow_sc = rest[1:] if aliased else rest
    kw = NSA_KVH * HEAD_DIM
    wb = win_ref.shape[0]
    wout_ref[pl.ds(0, wb - 1)] = win_ref[pl.ds(1, wb - 1)]
    for kind in range(2):
        for g in range(NSA_KVH):
            lane0 = (4 + kind) * kw + g * HEAD_DIM
            wout_ref[wb - 1, kind, pl.ds(g, 1), :] = kv_ref[:, lane0:lane0 + HEAD_DIM]
    qbd = _bf(_qbd(q_ref[...], NSA_REP))
    s = _dot_nt(_bf(_heads_on_lanes(win_ref, 0, NSA_KVH)), qbd) * SCALE
    dist = pos - (pos - wb + lax.broadcasted_iota(jnp.int32, s.shape, 0))
    s_sc[...] = jnp.where(dist >= 0, jnp.where(dist <= WINDOW, s, NEG), NEG)
    k_new = kv_ref[:, 4 * kw:5 * kw]
    v_new = kv_ref[:, 5 * kw:6 * kw]
    s_new = _dot_nt(_bf(jnp.broadcast_to(k_new, (8, kw))), qbd)[0:1] * SCALE
    new_ok = jnp.ones(s_new.shape, jnp.float32) > 0.5
    _softmax_keys_into(s_sc, s_new, new_ok, ow_sc, [win_ref], v_new, NSA_REP, wb)
    g = jax.nn.sigmoid(gate_ref[...])
    o_ref[...] = (g[0] * ocmp_ref[...] + g[1] * oslc_ref[...] + g[2] * ow_sc[...]).astype(o_ref.dtype)


def _layer_slot(prev):
    if prev is None:
        return [], [], False
    return [prev], [pl.BlockSpec(memory_space=pl.ANY)], True


def nsa_win_decode(cache_win, layer, q_s, kv_new, o_cmp, o_slc, gate_cols, pos, win_out_prev):
    b, wb = cache_win.shape[1], cache_win.shape[2]
    hspec = pl.BlockSpec((None, NSA_H, HEAD_DIM), lambda i: (i, 0, 0))
    wspec = pl.BlockSpec((None, None, wb, 2, NSA_KVH, HEAD_DIM), lambda i: (layer, i, 0, 0, 0, 0))
    extra, extra_specs, aliased = _layer_slot(win_out_prev)
    return pl.pallas_call(
        functools.partial(_win_decode_kernel, pos=pos, aliased=aliased),
        grid=(b,),
        in_specs=[wspec, hspec, pl.BlockSpec((None, 1, N_NKV), lambda i: (i, 0, 0)), hspec, hspec,
                  pl.BlockSpec((None, 3, NSA_H, 1), lambda i: (i, 0, 0, 0))] + extra_specs,
        out_specs=[hspec, wspec],
        out_shape=[jax.ShapeDtypeStruct((b, NSA_H, HEAD_DIM), jnp.bfloat16),
                   jax.ShapeDtypeStruct(cache_win.shape, cache_win.dtype)],
        scratch_shapes=[pltpu.VMEM((wb, 128), jnp.float32), pltpu.VMEM((NSA_H, HEAD_DIM), jnp.float32)],
        input_output_aliases={6: 1} if aliased else {},
        compiler_params=_cparams(("parallel",)),
        name="nsa_win_decode",
    )(cache_win, q_s, kv_new, o_cmp, o_slc, gate_cols, *extra)


def _moba_decode_kernel(pt_ref, *refs, n_pages, pos):
    pages = refs[:n_pages]
    q_ref, kn_ref, vn_ref, o_ref, s_sc, pn_sc = refs[n_pages:]
    phase = pl.program_id(1)
    per_blk = MOBA_BLOCK // PAGE_SIZE
    nb = n_pages // per_blk

    @pl.when(phase == 0)
    def _():
        qf = _qbd(q_ref[...], 1)
        qbd = _bf(qf)
        ksum = []
        for p in range(n_pages):
            k = jnp.concatenate([_stream(pages[p], h) for h in range(MOBA_H)], axis=1)
            s_sc[p * PAGE_SIZE:(p + 1) * PAGE_SIZE, :] = _dot_nt(_bf(k), qbd) * SCALE
            ksum.append(jnp.sum(k, axis=0, keepdims=True))
        kmean = jnp.concatenate(
            [sum(ksum[n * per_blk:(n + 1) * per_blk]) * (1.0 / MOBA_BLOCK) for n in range(nb)], axis=0)
        blk = lax.broadcasted_iota(jnp.int32, (nb, 128), 0)
        cur = pos // MOBA_BLOCK
        gate = jnp.where(blk < cur, _dot_nt_f32(kmean, qf), NEG)
        rank = _rank_sublanes(gate, nb)
        sel = jnp.where(rank < float(min(cur, MOBA_TOPK, nb)), 1.0, 0.0)
        row = lax.broadcasted_iota(jnp.int32, (PAGE_SIZE, 128), 0)
        for p in range(n_pages):
            ok = (jnp.broadcast_to(sel[p // per_blk:p // per_blk + 1, :], (PAGE_SIZE, 128)) > 0.5)
            kpos = p * PAGE_SIZE + row
            rows = slice(p * PAGE_SIZE, (p + 1) * PAGE_SIZE)
            s_sc[rows, :] = jnp.where(kpos <= pos, jnp.where(ok, s_sc[rows, :], NEG), NEG)
        s_new = _dot_nt(_bf(jnp.broadcast_to(kn_ref[...], (8, MIX_W))), qbd)[0:1] * SCALE
        new_ok = jnp.ones(s_new.shape, jnp.float32) > 0.5
        p_all, p_new = _softmax_keys(s_sc, s_new, new_ok)
        s_sc[...] = p_all
        pn_sc[...] = jnp.broadcast_to(p_new, pn_sc.shape)

    @pl.when(phase == 1)
    def _():
        acc = jnp.zeros((MOBA_H, MIX_W), jnp.float32)
        for p in range(n_pages):
            v_all = _bf(jnp.concatenate([_stream(pages[p], h) for h in range(MOBA_H)], axis=1))
            pv = lax.dot_general(_bf(s_sc[p * PAGE_SIZE:(p + 1) * PAGE_SIZE, :]), v_all, (((0,), (0,)), ((), ())),
                                 preferred_element_type=jnp.float32)
            acc = acc + pv[0:MOBA_H]
        p_new, v_new = pn_sc[0:1, :], vn_ref[...]
        o_ref[...] = jnp.concatenate(
            [acc[h:h + 1, h * HEAD_DIM:(h + 1) * HEAD_DIM]
             + p_new[:, h:h + 1] * v_new[:, h * HEAD_DIM:(h + 1) * HEAD_DIM] for h in range(MOBA_H)],
            axis=0).astype(o_ref.dtype)


def moba_decode(cache_moba, layer, pt_flat, q_m, k_new, v_new, n_pages):
    b = q_m.shape[0]
    pos = n_pages * PAGE_SIZE
    grid_spec = pltpu.PrefetchScalarGridSpec(
        num_scalar_prefetch=1, grid=(b, 2),
        in_specs=_page_specs(n_pages, layer, MOBA_H, phased=True)
        + [pl.BlockSpec((None, MOBA_H, HEAD_DIM), lambda i, ph, pt: (i, 0, 0)),
           pl.BlockSpec((None, 1, MIX_W), lambda i, ph, pt: (i, 0, 0)),
           pl.BlockSpec((None, 1, MIX_W), lambda i, ph, pt: (i, 0, 0))],
        out_specs=pl.BlockSpec((None, MOBA_H, HEAD_DIM), lambda i, ph, pt: (i, 0, 0)),
        scratch_shapes=[pltpu.VMEM((n_pages * PAGE_SIZE, 128), jnp.float32), pltpu.VMEM((8, 128), jnp.float32)])
    return pl.pallas_call(
        functools.partial(_moba_decode_kernel, n_pages=n_pages, pos=pos),
        grid_spec=grid_spec,
        out_shape=jax.ShapeDtypeStruct((b, MOBA_H, HEAD_DIM), jnp.bfloat16),
        compiler_params=_cparams(("parallel", "arbitrary")),
        name="moba_decode",
    )(pt_flat, *([cache_moba] * n_pages), q_m, k_new, v_new)


def _hgrn_decode_kernel(s_ref, hqf_ref, hi_ref, hg_ref, lb_ref, nw_ref, *rest, aliased):
    o_ref, so_ref = rest[1:] if aliased else rest
    mine = lax.broadcasted_iota(jnp.int32, hqf_ref.shape[2:], 1) == pl.program_id(0)
    col = lambda x: jnp.sum(jnp.where(mine, x, 0.0), axis=1, keepdims=True)
    for h in range(HG_H):
        lanes = slice(h * HG_DV, (h + 1) * HG_DV)
        q, k, logf = _hgrn_gates(col(hqf_ref[0, h]), col(hqf_ref[1, h]), lb_ref[h])
        s_new = jnp.exp(logf) * s_ref[h] + k * hi_ref[:, lanes]
        so_ref[h] = s_new
        o = jnp.sum(q * s_new, axis=0, keepdims=True)
        o_ref[:, lanes] = _hgrn_out(o, hg_ref[:, lanes], nw_ref[:, lanes]).astype(o_ref.dtype)


def hgrn_decode(state, layer, hqf_t, hi_row, hg_row, lb, norm_w, state_out_prev):
    b = state.shape[1]
    rowspec = pl.BlockSpec((None, 1, MIX_W), lambda i: (i, 0, 0))
    sspec = pl.BlockSpec((None, None, HG_H, HG_DK, HG_DV), lambda i: (layer, i, 0, 0, 0))
    extra, extra_specs, aliased = _layer_slot(state_out_prev)
    return pl.pallas_call(
        functools.partial(_hgrn_decode_kernel, aliased=aliased),
        grid=(b,),
        in_specs=[sspec, pl.BlockSpec(hqf_t.shape, lambda i: (0, 0, 0, 0)), rowspec, rowspec,
                  pl.BlockSpec((HG_H, HG_DK, 1), lambda i: (0, 0, 0)),
                  pl.BlockSpec((1, MIX_W), lambda i: (0, 0))] + extra_specs,
        out_specs=[rowspec, sspec],
        out_shape=[jax.ShapeDtypeStruct((b, 1, MIX_W), jnp.bfloat16),
                   jax.ShapeDtypeStruct(state.shape, jnp.float32)],
        input_output_aliases={6: 1} if aliased else {},
        compiler_params=_cparams(("parallel",)),
        name="hgrn_decode",
    )(state, hqf_t, hi_row, hg_row, lb.reshape(HG_H, HG_DK, 1), norm_w.reshape(1, MIX_W), *extra)


def finish_layer(x2, z, o_nsa, o_hg, o_moba, lw):
    layer = lw["layer"]
    mixed = branch_merge((o_nsa, o_hg, o_moba), lw["w_branch"], layer, z)
    x2 = matmul(mixed, lw["w_out"], layer, epilogue="residual", residual=x2, tn=512)
    h2 = rmsnorm_rows(x2, lw["norm2_w"], jnp.bfloat16)
    u2 = matmul(h2, lw["w_up"], layer, epilogue="relu2", out_dtype=jnp.bfloat16)
    return matmul(u2, lw["w_down"], layer, epilogue="residual", residual=x2)


def prompt_layer(x, lb, lw):
    b, t, _ = x.shape
    x2 = x.reshape(b * t, D_MODEL)
    h = rmsnorm_rows(x2, lw["norm1_w"], jnp.bfloat16)
    z = project_in(h, lw["w_in_t"], lw["layer"])
    ckv = nsa_compress_prompt(z, b, t, lw["cw"])
    o_nsa = nsa_prompt(z, ckv, b, t)
    o_moba = moba_prompt(z, b, t)
    o_hg, s_fin = hgrn_prompt(z, lb, lw["hg_norm_w"], b, t)
    x2 = finish_layer(x2, z, o_nsa, o_hg, o_moba, lw)
    return (x2.reshape(b, t, D_MODEL),) + kv_cache_rows(z, b, t) + (s_fin,)


def kv_cache_rows(z, b, t):
    z3 = z.reshape(b, t, D_IN_PAD)
    gw = 2 * NSA_KVH * HEAD_DIM
    kv = lambda kind: z3[:, :, OFF_NKV + kind * gw:OFF_NKV + (kind + 1) * gw].reshape(b, t, 2, NSA_KVH, HEAD_DIM)
    mkv = z3[:, :, OFF_MQKV + MIX_W:OFF_MQKV + 3 * MIX_W].reshape(b, t, 2, MOBA_H, HEAD_DIM)
    return kv(0), kv(1), mkv, kv(2)[:, max(t - WINDOW, 0):]


def sample_layer(x, layer, caches, page_table, lb, lw, stacked_prev):
    b, t, _ = x.shape
    assert t == 1, "the decode kernels take one new token per sequence"
    x2 = x.reshape(b * t, D_MODEL)
    h = rmsnorm_rows(x2, lw["norm1_w"], jnp.bfloat16)
    z = project_in(h, lw["w_in_t"], lw["layer"])
    o_nsa, o_hg, o_moba, win_out, s_out = sample_mixers(z, layer, caches, page_table, lb, lw, stacked_prev)
    x2 = finish_layer(x2, z, o_nsa, o_hg, o_moba, lw)
    c_cmp, c_slc, c_moba, _ = kv_cache_rows(z, b, t)
    return (x2.reshape(b, t, D_MODEL), c_cmp, c_slc, c_moba), (win_out, s_out)


def sample_mixers(z, layer, caches, page_table, lb, lw, stacked_prev=(None, None)):
    cache_cmp, cache_slc, cache_moba, cache_win, state = caches
    win_prev, state_prev = stacked_prev
    b = z.shape[0]
    n_pages = page_table.shape[1]
    pos = n_pages * PAGE_SIZE
    pt_flat = page_table.reshape(-1)
    col = lambda off, n: z[:, off:off + n]
    q_s = col(OFF_NQ, N_NQ).reshape(b, NSA_H, HEAD_DIM)
    kv_new = col(OFF_NKV, N_NKV).reshape(b, 1, N_NKV)
    gate_cols = col(OFF_NGATE, N_NGATE).reshape(b, 3, NSA_H, 1)
    o_cmp, sel = nsa_cmp_decode(cache_cmp, layer, pt_flat, q_s, lw["cw"], n_pages)
    o_slc = nsa_slc_decode(cache_slc, layer, pt_flat, q_s, kv_new, sel, n_pages)
    o_nsa, win_out = nsa_win_decode(cache_win, layer, q_s, kv_new, o_cmp, o_slc, gate_cols, pos, win_prev)
    q_m = col(OFF_MQKV, MIX_W).reshape(b, MOBA_H, HEAD_DIM)
    k_new = col(OFF_MQKV + MIX_W, MIX_W).reshape(b, 1, MIX_W)
    v_new = col(OFF_MQKV + 2 * MIX_W, MIX_W).reshape(b, 1, MIX_W)
    o_moba = moba_decode(cache_moba, layer, pt_flat, q_m, k_new, v_new, n_pages)
    hqf_t = col(OFF_HQ, 2 * N_HG).T.reshape(2, HG_H, HG_DK, b)
    o_hg, s_out = hgrn_decode(state, layer, hqf_t, col(OFF_HI, N_HG).reshape(b, 1, MIX_W),
                              col(OFF_HGATE, N_HG).reshape(b, 1, MIX_W), lb, lw["hg_norm_w"], state_prev)
    return o_nsa.reshape(b, MIX_W), o_hg.reshape(b, MIX_W), o_moba.reshape(b, MIX_W), win_out, s_out


def prep_layer_weights(i, norm1_w, norm2_w, w_in, nsa_cmp_pos, nsa_cmp_w1, nsa_cmp_w2, hgrn_norm_w,
                       w_branch, w_out, w_up, w_down):
    assert w_in.shape[2] - (N_NQ + N_NKV + N_NGATE) == OFF_NGATE - (N_NQ + N_NKV)
    return dict(layer=i, norm1_w=norm1_w[i], norm2_w=norm2_w[i], w_in_t=jnp.swapaxes(w_in, 1, 2),
                cw=prep_compress_weights(nsa_cmp_pos[i], nsa_cmp_w1[i], nsa_cmp_w2[i]), hg_norm_w=hgrn_norm_w[i],
                w_branch=w_branch, w_out=w_out, w_up=w_up, w_down=w_down.astype(jnp.bfloat16))


def kernel(x_prompt, x_sample, cache_nsa_cmp, cache_nsa_slc, cache_moba, cache_nsa_win, state_hgrn,
           page_table, norm1_w, norm2_w, w_in, nsa_cmp_pos, nsa_cmp_w1, nsa_cmp_w2, hgrn_lb_logits,
           hgrn_norm_w, w_branch, w_out, w_up, w_down, final_norm_w):
    sm = jax.nn.softmax(hgrn_lb_logits.astype(jnp.float32), axis=0)
    lbs = jnp.cumsum(sm, axis=0) - sm[0:1]
    xp, xs = x_prompt, x_sample
    outs_p, outs_s = [], []
    stacked = (None, None)
    for i in range(DEPTH):
        lw = prep_layer_weights(i, norm1_w, norm2_w, w_in, nsa_cmp_pos, nsa_cmp_w1, nsa_cmp_w2,
                                hgrn_norm_w, w_branch, w_out, w_up, w_down)
        xp, *st_p = prompt_layer(xp, lbs[i], lw)
        (xs, *st_s), stacked = sample_layer(
            xs, i, (cache_nsa_cmp, cache_nsa_slc, cache_moba, cache_nsa_win, state_hgrn), page_table, lbs[i], lw,
            stacked)
        outs_p.append(st_p)
        outs_s.append(st_s)
    y_prompt = rmsnorm_rows(xp.reshape(-1, D_MODEL), final_norm_w, jnp.float32).reshape(xp.shape)
    y_sample = rmsnorm_rows(xs.reshape(-1, D_MODEL), final_norm_w, jnp.float32).reshape(xs.shape)
    stack = lambda outs, j: jnp.stack([o[j] for o in outs])
    return (y_prompt, y_sample) + tuple(stack(outs_p, j) for j in range(5)) + tuple(
        stack(outs_s, j) for j in range(3)) + stacked
```

```python
import functools

import jax
import jax.numpy as jnp
import numpy as np
from jax import lax
from jax.experimental import pallas as pl
from jax.experimental.pallas import tpu as pltpu

D_MODEL = 2048
DEPTH = 2
PAGE_SIZE = 128
HEAD_DIM = 128
MIX_W = D_MODEL // 2
N_BRANCH = 3
NSA_H = MIX_W // HEAD_DIM
NSA_KVH = NSA_H // 4
CMP_LEN = 32
CMP_STRIDE = 16
SEL_BLOCK = 64
SEL_TOPK = 16
WINDOW = 512
FORCE_BONUS = 1e4
MOBA_H = MIX_W // HEAD_DIM
MOBA_BLOCK = 256
MOBA_TOPK = 3
HG_H = MIX_W // HEAD_DIM
HG_DK = 128
HG_DV = MIX_W // HG_H
EPS = 1e-6
F_FLOOR = 1e-30
NEG = -1e30

N_NQ = NSA_H * HEAD_DIM
N_NKV = 6 * NSA_KVH * HEAD_DIM
N_NGATE = 3 * NSA_H
N_MQKV = 3 * MOBA_H * HEAD_DIM
N_HG = HG_H * HG_DK
N_MG = N_BRANCH * D_MODEL
OFF_NQ = 0
OFF_NKV = OFF_NQ + N_NQ
OFF_MQKV = OFF_NKV + N_NKV
OFF_HQ = OFF_MQKV + N_MQKV
OFF_HF = OFF_HQ + N_HG
OFF_HI = OFF_HF + N_HG
OFF_HGATE = OFF_HI + N_HG
OFF_MG = OFF_HGATE + N_HG
OFF_NGATE = OFF_MG + N_MG
D_IN_PAD = 16384

VMEM_LIMIT_BYTES = 48 * 1024 * 1024


def _cparams(sem):
    return pltpu.CompilerParams(dimension_semantics=sem, vmem_limit_bytes=VMEM_LIMIT_BYTES)


def _rmsnorm_kernel(x_ref, w_ref, o_ref):
    x = x_ref[...]
    y = x * lax.rsqrt(jnp.mean(x * x, axis=-1, keepdims=True) + EPS)
    o_ref[...] = (y * w_ref[...]).astype(o_ref.dtype)


def rmsnorm_rows(x, w, out_dtype):
    m, d = x.shape
    tm = min(m, 512)
    return pl.pallas_call(
        _rmsnorm_kernel,
        grid=(m // tm,),
        in_specs=[pl.BlockSpec((tm, d), lambda i: (i, 0)), pl.BlockSpec((1, d), lambda i: (0, 0))],
        out_specs=pl.BlockSpec((tm, d), lambda i: (i, 0)),
        out_shape=jax.ShapeDtypeStruct((m, d), out_dtype),
        compiler_params=_cparams(("parallel",)),
        name="rmsnorm",
    )(x, w.reshape(1, d))


def _mm_kernel(*refs, epilogue, nk):
    if epilogue == "residual":
        a_ref, w_ref, r_ref, o_ref = refs[:4]
        rest = refs[4:]
    else:
        a_ref, w_ref, o_ref = refs[:3]
        r_ref = None
        rest = refs[3:]
    part = jnp.dot(a_ref[...], w_ref[...].astype(jnp.bfloat16), preferred_element_type=jnp.float32)

    def finish(acc):
        if epilogue == "residual":
            o_ref[...] = r_ref[...] + acc
        elif epilogue == "relu2":
            u = jnp.maximum(acc, 0.0)
            o_ref[...] = (u * u).astype(o_ref.dtype)
        else:
            o_ref[...] = acc.astype(o_ref.dtype)

    if nk == 1:
        finish(part)
    else:
        acc_ref = rest[0]
        k = pl.program_id(2)

        @pl.when(k == 0)
        def _():
            acc_ref[...] = part

        @pl.when(k > 0)
        def _():
            acc_ref[...] += part

        @pl.when(k == nk - 1)
        def _():
            finish(acc_ref[...])


def matmul(a, w, layer, *, epilogue="none", residual=None, out_dtype=jnp.float32, tm=1024, tn=1024, tk=2048):
    m, kdim = a.shape
    n = w.shape[2]
    tm, tn, tk = min(tm, m), min(tn, n), min(tk, kdim)
    nk = kdim // tk
    in_specs = [pl.BlockSpec((tm, tk), lambda j, i, k: (i, k)),
                pl.BlockSpec((None, tk, tn), lambda j, i, k: (layer, k, j))]
    args = [a, w]
    if epilogue == "residual":
        in_specs.append(pl.BlockSpec((tm, tn), lambda j, i, k: (i, j)))
        args.append(residual)
    scratch = [pltpu.VMEM((tm, tn), jnp.float32)] if nk > 1 else []
    return pl.pallas_call(
        functools.partial(_mm_kernel, epilogue=epilogue, nk=nk),
        grid=(n // tn, m // tm, nk),
        in_specs=in_specs,
        out_specs=pl.BlockSpec((tm, tn), lambda j, i, k: (i, j)),
        out_shape=jax.ShapeDtypeStruct((m, n), out_dtype),
        scratch_shapes=scratch,
        compiler_params=_cparams(("parallel", "parallel", "arbitrary")),
        name="matmul_" + epilogue,
    )(*args)


W_IN_TN = 512


def _w_in_row(j):
    n_lo = (N_NQ + N_NKV) // W_IN_TN
    gate_tile = OFF_NGATE // W_IN_TN
    row = jnp.where(j < n_lo, j * W_IN_TN,
                    jnp.where(j < gate_tile, N_NQ + N_NKV + N_NGATE + (j - n_lo) * W_IN_TN, N_NQ + N_NKV))
    return pl.multiple_of(row, 8)


def _project_kernel(a_ref, wt_ref, o_ref):
    o_ref[...] = lax.dot_general(a_ref[...], wt_ref[0].astype(jnp.bfloat16), (((1,), (1,)), ((), ())),
                                 preferred_element_type=jnp.float32)


def project_in(a, w_in_t, layer, *, tm=2048):
    m, d = a.shape
    assert (N_NQ + N_NKV) % W_IN_TN == 0 and OFF_NGATE % W_IN_TN == 0 and D_IN_PAD == OFF_NGATE + W_IN_TN
    tm = min(tm, m)
    return pl.pallas_call(
        _project_kernel,
        grid=(m // tm, D_IN_PAD // W_IN_TN),
        in_specs=[pl.BlockSpec((tm, d), lambda i, j: (i, 0)),
                  pl.BlockSpec((pl.Element(1), pl.Element(W_IN_TN), pl.Element(d)),
                               lambda i, j: (layer, _w_in_row(j), 0))],
        out_specs=pl.BlockSpec((tm, W_IN_TN), lambda i, j: (i, j)),
        out_shape=jax.ShapeDtypeStruct((m, D_IN_PAD), jnp.float32),
        compiler_params=_cparams(("parallel", "parallel")),
        name="project_in",
    )(a, w_in_t)


def _merge_kernel(b0_ref, b1_ref, b2_ref, wb_ref, g0_ref, g1_ref, g2_ref, o_ref):
    acc = None
    for n, (b_ref, g_ref) in enumerate(((b0_ref, g0_ref), (b1_ref, g1_ref), (b2_ref, g2_ref))):
        proj = jnp.dot(b_ref[...], wb_ref[n].astype(jnp.bfloat16), preferred_element_type=jnp.float32)
        term = jax.nn.sigmoid(g_ref[...]) * proj
        acc = term if acc is None else acc + term
    o_ref[...] = acc.astype(o_ref.dtype)


def branch_merge(branches, wb, layer, z, *, tm=512, tn=512):
    m = branches[0].shape[0]
    tm = min(tm, m)
    gate_specs = [
        pl.BlockSpec((tm, tn), lambda j, i, n=n: (i, (OFF_MG + n * D_MODEL) // tn + j)) for n in range(N_BRANCH)]
    return pl.pallas_call(
        _merge_kernel,
        grid=(D_MODEL // tn, m // tm),
        in_specs=[pl.BlockSpec((tm, MIX_W), lambda j, i: (i, 0))] * N_BRANCH
        + [pl.BlockSpec((None, N_BRANCH, MIX_W, tn), lambda j, i: (layer, 0, 0, j))] + gate_specs,
        out_specs=pl.BlockSpec((tm, tn), lambda j, i: (i, j)),
        out_shape=jax.ShapeDtypeStruct((m, D_MODEL), jnp.bfloat16),
        compiler_params=_cparams(("parallel", "parallel")),
        name="branch_merge",
    )(*branches, wb, z, z, z)


SCALE = HEAD_DIM ** -0.5
NSA_REP = NSA_H // NSA_KVH
CMP_PER_PAGE = PAGE_SIZE // CMP_STRIDE


def _bf(x):
    return x.astype(jnp.bfloat16)


def _split3(x):
    hi = _bf(x)
    r1 = x - hi.astype(jnp.float32)
    mid = _bf(r1)
    lo = _bf(r1 - mid.astype(jnp.float32))
    return hi, mid, lo


_NT = (((1,), (1,)), ((), ()))


def _dot(a, b):
    return jnp.dot(a, b, preferred_element_type=jnp.float32)


def _dot_nt(a, b):
    return lax.dot_general(a, b, _NT, preferred_element_type=jnp.float32)


def _dot_f32_lhs(a, b_exact):
    return sum(_dot(p, b_exact) for p in _split3(a))


def _dot_f32_rhs(a_exact, b):
    return sum(_dot(a_exact, p) for p in _split3(b))


def _dot_nt_f32(a, b):
    a1, a2, a3 = _split3(a)
    b1, b2, b3 = _split3(b)
    return (_dot_nt(a1, b1) + (_dot_nt(a1, b2) + _dot_nt(a2, b1))
            + (_dot_nt(a1, b3) + _dot_nt(a2, b2) + _dot_nt(a3, b1)))


def _masked_exp(s, mask):
    s = jnp.where(mask, s, NEG)
    m = jnp.max(s, axis=-1, keepdims=True)
    e = jnp.where(mask, jnp.exp(s - m), 0.0)
    l = jnp.sum(e, axis=-1, keepdims=True)
    return e, 1.0 / jnp.where(l > 0.0, l, 1.0)


def _masked_softmax_rows(s, mask):
    s = jnp.where(mask, s, NEG)
    m = jnp.max(s, axis=0, keepdims=True)
    e = jnp.where(mask, jnp.exp(s - m), 0.0)
    l = jnp.sum(e, axis=0, keepdims=True)
    return e / jnp.where(l > 0.0, l, 1.0)


def _rank_lanes(score, n):
    lane = lax.broadcasted_iota(jnp.int32, score.shape, 1)
    rank = jnp.zeros(score.shape, jnp.float32)
    for i in range(n):
        col = score[:, i:i + 1]
        ahead = jnp.where(col == score, jnp.where(lane > i, 1.0, 0.0), jnp.where(col > score, 1.0, 0.0))
        rank = rank + ahead
    return rank


def _rank_sublanes(score, n):
    row = lax.broadcasted_iota(jnp.int32, score.shape, 0)
    rank = jnp.zeros(score.shape, jnp.float32)
    for i in range(n):
        r = score[i:i + 1, :]
        ahead = jnp.where(r == score, jnp.where(row > i, 1.0, 0.0), jnp.where(r > score, 1.0, 0.0))
        rank = rank + ahead
    return rank


def _compress_tokens(x_bf, w1ab_ref, w2_ref, posrows_ref):
    n = x_bf.shape[0]
    w1ab = w1ab_ref[...]
    pre = _dot(x_bf, w1ab)
    pb = _dot(posrows_ref[...], w1ab)
    posbias = pb[0:1, :HEAD_DIM] + pb[1:2, HEAD_DIM:]
    nxt = pltpu.roll(pre[:, HEAD_DIM:], n - 1, 0)
    hid = pre[:, :HEAD_DIM] + nxt + posbias
    return _dot(_bf(jax.nn.silu(hid)), w2_ref[...])


def prep_compress_weights(cmp_pos, cmp_w1, cmp_w2):
    half = CMP_STRIDE * HEAD_DIM
    w1ab = jnp.concatenate([cmp_w1[:, :half], cmp_w1[:, half:]], axis=-1)
    pos2 = cmp_pos.reshape(2, 2, half)
    posrows = jnp.concatenate([pos2, jnp.zeros((2, 6, half), cmp_pos.dtype)], axis=1)
    return _bf(w1ab), _bf(cmp_w2), _bf(posrows)


def _cmp_prompt_kernel(x_ref, w1ab_ref, w2_ref, posrows_ref, o_ref):
    n_chunks = x_ref.shape[0] // CMP_STRIDE
    xc = jnp.concatenate(
        [_bf(x_ref[pl.ds(j, n_chunks, stride=CMP_STRIDE), :]) for j in range(CMP_STRIDE)], axis=-1)
    o_ref[...] = _compress_tokens(xc, w1ab_ref, w2_ref, posrows_ref)


def nsa_compress_prompt(z, b, t, cw):
    w1ab, w2, posrows = cw
    n_chunks = t // CMP_STRIDE
    kind_spec = lambda a: pl.BlockSpec((None,) + a.shape[1:], lambda i, c: (c // NSA_KVH, 0, 0))
    return pl.pallas_call(
        _cmp_prompt_kernel,
        grid=(b, 2 * NSA_KVH),
        in_specs=[pl.BlockSpec((t, HEAD_DIM), lambda i, c: (i, OFF_NKV // HEAD_DIM + c)),
                  kind_spec(w1ab), kind_spec(w2), kind_spec(posrows)],
        out_specs=pl.BlockSpec((None, None, n_chunks, HEAD_DIM), lambda i, c: (i, c, 0, 0)),
        out_shape=jax.ShapeDtypeStruct((b, 2 * NSA_KVH, n_chunks, HEAD_DIM), jnp.float32),
        compiler_params=_cparams(("parallel", "parallel")),
        name="nsa_compress_prompt",
    )(z, w1ab, w2, posrows)


NSA_TQ = 128
NSA_KEY_SPANS = 4


def _nsa_prompt_kernel(q_ref, ckv_ref, ks_ref, vs_ref, kw_ref, vw_ref, gate_ref, ovl_ref, exp_ref, o_ref,
                       ks_bf, vs_bf, kw_bf, vw_bf, oslc_sc, *, t):
    qt = pl.program_id(1)
    q0 = pl.multiple_of(qt * NSA_TQ, NSA_TQ)
    n_cmp = ckv_ref.shape[1]
    n_sel = t // SEL_BLOCK
    span = WINDOW + NSA_TQ
    nqt = t // NSA_TQ
    n_span = min(NSA_KEY_SPANS, nqt)

    @pl.when(qt == 0)
    def _():
        ks_bf[...] = _bf(ks_ref[...])
        vs_bf[...] = _bf(vs_ref[...])
        kw_bf[...] = _bf(kw_ref[...])
        vw_bf[...] = _bf(vw_ref[...])

    qpos = q0 + lax.broadcasted_iota(jnp.int32, (NSA_TQ, 1), 0)
    gates = jax.nn.sigmoid(gate_ref[...])
    lane = lax.broadcasted_iota(jnp.int32, (NSA_TQ, 128), 1)
    lane_c = lax.broadcasted_iota(jnp.int32, (NSA_TQ, n_cmp), 1)
    cmask = (lane_c * CMP_STRIDE + (CMP_LEN - 1) <= qpos) & (lane_c < n_cmp - 1)
    cur = qpos // SEL_BLOCK
    forced = (lane == 0) | (lane == cur) | (lane == cur - 1)
    wstart =pl.multiple_of(jnp.maximum(q0 - WINDOW, 0), NSA_TQ)
    wd = qpos - (wstart + lax.broadcasted_iota(jnp.int32, (NSA_TQ, span), 1))
    wmask = (wd >= 0) & (wd <= WINDOW)

    def attend(qs, k_bf, v_bf, mask):
        nk = k_bf.shape[0]
        s = _dot_nt(qs, k_bf).reshape(NSA_REP, NSA_TQ, nk) * SCALE
        e, inv = _masked_exp(s, mask[None])
        o = _dot(_bf(e.reshape(NSA_REP * NSA_TQ, nk)), v_bf) * inv.reshape(NSA_REP * NSA_TQ, 1)
        return e * inv, o

    for g in range(NSA_KVH):
        qs = _bf(jnp.concatenate(
            [q_ref[:, (g * NSA_REP + r) * HEAD_DIM:(g * NSA_REP + r + 1) * HEAD_DIM] for r in range(NSA_REP)],
            axis=0))
        p_cmp, o_cmp = attend(qs, _bf(ckv_ref[g]), _bf(ckv_ref[NSA_KVH + g]), cmask)
        psum = p_cmp[0] + p_cmp[1] + p_cmp[2] + p_cmp[3]
        imp = _dot_f32_lhs(psum, ovl_ref[...])
        score = jnp.where(lane <= cur, imp + jnp.where(forced, FORCE_BONUS, 0.0), NEG)
        rank = _rank_lanes(score, n_sel)
        n_ok = jnp.minimum(cur + 1, min(SEL_TOPK, n_sel)).astype(jnp.float32)
        sel = jnp.where(rank < n_ok, 1.0, 0.0)
        sel_b = _bf(sel)
        gsl = slice(g * HEAD_DIM, (g + 1) * HEAD_DIM)

        def selected(nk, sel_b=sel_b, qs=qs, gsl=gsl):
            keysel = _dot(sel_b, exp_ref[:, 0:nk])
            kpos = lax.broadcasted_iota(jnp.int32, (NSA_TQ, nk), 1)
            smask = jnp.where(kpos <= qpos, keysel, 0.0) > 0.5
            oslc_sc[...] = attend(qs, ks_bf[0:nk, gsl], vs_bf[0:nk, gsl], smask)[1]

        for j in range(n_span):
            pl.when(qt // (nqt // n_span) == j)(functools.partial(selected, (j + 1) * (t // n_span)))
        o_slc = oslc_sc[...]
        _, o_win = attend(qs, kw_bf[pl.ds(wstart, span), gsl], vw_bf[pl.ds(wstart, span), gsl], wmask)
        for r in range(NSA_REP):
            h = g * NSA_REP + r
            rows = slice(r * NSA_TQ, (r + 1) * NSA_TQ)
            o = (gates[:, h:h + 1] * o_cmp[rows] + gates[:, NSA_H + h:NSA_H + h + 1] * o_slc[rows]
                 + gates[:, 2 * NSA_H + h:2 * NSA_H + h + 1] * o_win[rows])
            o_ref[:, h * HEAD_DIM:(h + 1) * HEAD_DIM] = o.astype(o_ref.dtype)


def nsa_prompt(z, ckv, b, t):
    n_cmp = ckv.shape[2]
    n_sel = t // SEL_BLOCK
    nqt = t // NSA_TQ
    gw = NSA_KVH * HEAD_DIM
    cmp_start = np.arange(n_cmp) * CMP_STRIDE
    sel_start = np.arange(128) * SEL_BLOCK
    ovl = ((cmp_start[:, None] < sel_start[None, :] + SEL_BLOCK) & (cmp_start[:, None] + CMP_LEN > sel_start[None, :])
           & (np.arange(n_cmp)[:, None] < n_cmp - 1) & (np.arange(128)[None, :] < n_sel))
    ovl = jnp.asarray(ovl, jnp.bfloat16)
    expand = jnp.asarray(np.arange(128)[:, None] == (np.arange(t)[None, :] // SEL_BLOCK), jnp.bfloat16)
    kv_spec = lambda kind: pl.BlockSpec((t, gw), lambda i, j: (i, OFF_NKV // gw + kind))
    return pl.pallas_call(
        functools.partial(_nsa_prompt_kernel, t=t),
        grid=(b, nqt),
        in_specs=[pl.BlockSpec((NSA_TQ, N_NQ), lambda i, j: (i * nqt + j, OFF_NQ // N_NQ)),
                  pl.BlockSpec((None, 2 * NSA_KVH, n_cmp, HEAD_DIM), lambda i, j: (i, 0, 0, 0)),
                  kv_spec(2), kv_spec(3), kv_spec(4), kv_spec(5),
                  pl.BlockSpec((NSA_TQ, 128), lambda i, j: (i * nqt + j, OFF_NGATE // 128)),
                  pl.BlockSpec(ovl.shape, lambda i, j: (0, 0)),
                  pl.BlockSpec(expand.shape, lambda i, j: (0, 0))],
        out_specs=pl.BlockSpec((NSA_TQ, MIX_W), lambda i, j: (i * nqt + j, 0)),
        out_shape=jax.ShapeDtypeStruct((b * t, MIX_W), jnp.bfloat16),
        scratch_shapes=[pltpu.VMEM((t, gw), jnp.bfloat16)] * 4
        + [pltpu.VMEM((NSA_REP * NSA_TQ, HEAD_DIM), jnp.float32)],
        compiler_params=_cparams(("parallel", "arbitrary")),
        name="nsa_prompt",
    )(z, ckv, z, z, z, z, z, ovl, expand)


def _moba_prompt_kernel(q_ref, k_ref, v_ref, exp_ref, o_ref, k_bf, v_bf, km_ref, *, t):
    qt = pl.program_id(2)
    nb = t // MOBA_BLOCK

    @pl.when(qt == 0)
    def _():
        k = k_ref[...]
        k_bf[...] = _bf(k)
        v_bf[...] = _bf(v_ref[...])
        km_ref[...] = jnp.zeros(km_ref.shape, jnp.float32)
        km_ref[0:nb, :] = jnp.mean(k.reshape(nb, MOBA_BLOCK, HEAD_DIM), axis=1)

    q = q_ref[...]
    qpos = qt * MOBA_BLOCK + lax.broadcasted_iota(jnp.int32, (MOBA_BLOCK, 1), 0)
    lane = lax.broadcasted_iota(jnp.int32, (MOBA_BLOCK, 128), 1)
    gate = jnp.where(lane < qt, _dot_nt_f32(q, km_ref[...]), NEG)
    rank = _rank_lanes(gate, nb)
    n_ok = jnp.minimum(qt, min(MOBA_TOPK, nb)).astype(jnp.float32)
    sel = _bf(jnp.where(rank < n_ok, 1.0, 0.0))
    q_bf = _bf(q)

    def attend(nk):
        keysel = _dot(sel, exp_ref[:, 0:nk])
        kpos = lax.broadcasted_iota(jnp.int32, (MOBA_BLOCK, nk), 1)
        own = (kpos // MOBA_BLOCK == qt) & (kpos <= qpos)
        mask = jnp.where(own, 1.0, keysel) > 0.5
        s = _dot_nt(q_bf, k_bf[0:nk, :]) * SCALE
        e, inv = _masked_exp(s, mask)
        o_ref[...] = (_dot(_bf(e), v_bf[0:nk, :]) * inv).astype(o_ref.dtype)

    for j in range(nb):
        pl.when(qt == j)(functools.partial(attend, (j + 1) * MOBA_BLOCK))


def moba_prompt(z, b, t):
    nqt = t // MOBA_BLOCK
    expand = jnp.asarray(np.arange(128)[:, None] == (np.arange(t)[None, :] // MOBA_BLOCK), jnp.bfloat16)
    col = lambda part: OFF_MQKV // HEAD_DIM + part * MOBA_H
    return pl.pallas_call(
        functools.partial(_moba_prompt_kernel, t=t),
        grid=(b, MOBA_H, nqt),
        in_specs=[pl.BlockSpec((MOBA_BLOCK, HEAD_DIM), lambda i, h, j: (i * nqt + j, col(0) + h)),
                  pl.BlockSpec((t, HEAD_DIM), lambda i, h, j: (i, col(1) + h)),
                  pl.BlockSpec((t, HEAD_DIM), lambda i, h, j: (i, col(2) + h)),
                  pl.BlockSpec(expand.shape, lambda i, h, j: (0, 0))],
        out_specs=pl.BlockSpec((MOBA_BLOCK, HEAD_DIM), lambda i, h, j: (i * nqt + j, h)),
        out_shape=jax.ShapeDtypeStruct((b * t, MIX_W), jnp.bfloat16),
        scratch_shapes=[pltpu.VMEM((t, HEAD_DIM), jnp.bfloat16), pltpu.VMEM((t, HEAD_DIM), jnp.bfloat16),
                        pltpu.VMEM((128, HEAD_DIM), jnp.float32)],
        compiler_params=_cparams(("parallel", "parallel", "arbitrary")),
        name="moba_prompt",
    )(z, z, z, expand)


HG_KCHUNK = 128
HG_SUB = 16
HG_HEADS_PER_STEP = 4


def _hgrn_gates(hq, hf, lb):
    q = jax.nn.silu(hq)
    f = lb + (1.0 - lb) * jax.nn.sigmoid(hf)
    logf = jnp.log(jnp.maximum(f, F_FLOOR))
    k = (1.0 - lb) * jax.nn.sigmoid(-hf)
    return q, k, logf


def _hgrn_out(o, hg, norm_w):
    o = o * lax.rsqrt(jnp.mean(o * o, axis=-1, keepdims=True) + EPS)
    return o * norm_w * jax.nn.silu(hg)


def _hgrn_prompt_kernel(hq_ref, hf_ref, hi_ref, hg_ref, lb_ref, nw_ref, tri_ref, o_ref, s_ref,
                        st_ref, kcv_sc, *, t):
    c = HG_KCHUNK
    st_ref[...] = jnp.zeros(st_ref.shape, jnp.float32)
    row_s = lax.broadcasted_iota(jnp.int32, (HG_SUB, 1), 0)
    col_c = lax.broadcasted_iota(jnp.int32, (HG_SUB, c), 1)

    def head_chunk(hh, rows):
        lanes = slice(hh * HG_DK, (hh + 1) * HG_DK)
        lb = lb_ref[:, lanes]
        nw = nw_ref[:, lanes]
        k_sc, cum_sc, v_sc = kcv_sc.at[hh, 0], kcv_sc.at[hh, 1], kcv_sc.at[hh, 2]
        q, k, logf = _hgrn_gates(hq_ref[rows, lanes], hf_ref[rows, lanes], lb)
        v = hi_ref[rows, lanes]
        cum = _dot_f32_rhs(tri_ref[...], logf)
        k_sc[...] = k
        cum_sc[...] = cum
        v_sc[...] = v
        v_b = _bf(v)
        st = st_ref[hh]
        o_carry = _dot_nt(_bf(q * jnp.exp(cum)), _bf(st))
        o_parts = []
        for i in range(c // HG_SUB):
            r0 = i * HG_SUB
            q_i = q[r0:r0 + HG_SUB]
            cum_i = cum[r0:r0 + HG_SUB]
            o_i = o_carry[r0:r0 + HG_SUB]
            if i > 0:
                edge = cum[r0 - 1:r0]
                k_e = k * jnp.exp(jnp.minimum(edge - cum, 0.0))
                a = _dot_nt(_bf(q_i * jnp.exp(cum_i - edge)), _bf(k_e))
                o_i = o_i + _dot(_bf(jnp.where(col_c < r0, a, 0.0)), v_b)
            for s in range(r0, r0 + HG_SUB):
                d = jnp.minimum(cum_i - cum_sc[pl.ds(s, 1), :], 0.0)
                w = jnp.sum(q_i * jnp.exp(d) * k_sc[pl.ds(s, 1), :], axis=-1, keepdims=True)
                o_i = o_i + jnp.where(row_s >= s - r0, w, 0.0) * v_sc[pl.ds(s, 1), :]
            o_parts.append(o_i)
        o = jnp.concatenate(o_parts, axis=0)
        o_ref[rows, lanes] = _hgrn_out(o, hg_ref[rows, lanes], nw).astype(o_ref.dtype)
        last = cum[c - 1:c]
        kd = k * jnp.exp(last - cum)
        upd = lax.dot_general(v_b, _bf(kd), (((0,), (0,)), ((), ())), preferred_element_type=jnp.float32)
        st_ref[hh] = st * jnp.exp(last) + upd

    def chunk(ci, carry):
        rows = pl.ds(pl.multiple_of(ci * c, c), c)
        for hh in range(HG_HEADS_PER_STEP):
            head_chunk(hh, rows)
        return carry

    lax.fori_loop(0, t // c, chunk, 0)
    for hh in range(HG_HEADS_PER_STEP):
        s_ref[hh] = st_ref[hh].T


def hgrn_prompt(z, lb, norm_w, b, t):
    tri = jnp.asarray(np.tril(np.ones((HG_KCHUNK, HG_KCHUNK))), jnp.bfloat16)
    hp = HG_HEADS_PER_STEP
    width = hp * HG_DK
    col = lambda off: (lambda i, h: (i, off // width + h))
    vec = pl.BlockSpec((1, width), lambda i, h: (0, h))
    return pl.pallas_call(
        functools.partial(_hgrn_prompt_kernel, t=t),
        grid=(b, HG_H // hp),
        in_specs=[pl.BlockSpec((t, width), col(OFF_HQ)), pl.BlockSpec((t, width), col(OFF_HF)),
                  pl.BlockSpec((t, width), col(OFF_HI)), pl.BlockSpec((t, width), col(OFF_HGATE)),
                  vec, vec, pl.BlockSpec(tri.shape, lambda i, h: (0, 0))],
        out_specs=[pl.BlockSpec((t, width), lambda i, h: (i, h)),
                   pl.BlockSpec((None, hp, HG_DK, HG_DV), lambda i, h: (i, h, 0, 0))],
        out_shape=[jax.ShapeDtypeStruct((b * t, MIX_W), jnp.bfloat16),
                   jax.ShapeDtypeStruct((b, HG_H, HG_DK, HG_DV), jnp.float32)],
        scratch_shapes=[pltpu.VMEM((hp, HG_DV, HG_DK), jnp.float32),
                        pltpu.VMEM((hp, 3, HG_KCHUNK, HG_DK), jnp.float32)],
        compiler_params=_cparams(("parallel", "parallel")),
        name="hgrn_prompt",
    )(z, z, z, z, lb.reshape(1, MIX_W), norm_w.reshape(1, MIX_W), tri)


SEL_ROWS = 40


def _qbd(q, heads_per_kv):
    n_kv = q.shape[0] // heads_per_kv
    row = lax.broadcasted_iota(jnp.int32, q.shape, 0)
    blocks = [jnp.where(row // heads_per_kv == g, q, 0.0) for g in range(n_kv)]
    top = jnp.concatenate(blocks, axis=1)
    return jnp.concatenate([top, jnp.zeros((128 - q.shape[0], top.shape[1]), jnp.float32)], axis=0)


def _softmax_keys(s_sc, s_new, new_ok):
    s = s_sc[...]
    s_new = jnp.where(new_ok, s_new, NEG)
    m = jnp.maximum(jnp.max(s, axis=0, keepdims=True), s_new)
    e = jnp.where(s > 0.5 * NEG, jnp.exp(s - m), 0.0)
    e_new = jnp.where(new_ok, jnp.exp(s_new - m), 0.0)
    l = jnp.sum(e, axis=0, keepdims=True) + e_new
    inv = 1.0 / jnp.where(l > 0.0, l, 1.0)
    return e * inv, e_new * inv


def _cmp_decode_kernel(pt_ref, *refs, n_pages, pos):
    pages = refs[:n_pages]
    q_ref, w1ab_ref, w2_ref, posrows_ref, ovlt_ref, grp_ref, perm_ref, o_ref, sel_ref, xs = refs[n_pages:]
    n_tok = n_pages * CMP_PER_PAGE
    pair = 2 * CMP_PER_PAGE
    for pp in range(n_pages // 2):
        for c in range(2 * NSA_KVH):
            rows = jnp.concatenate([_stream(pages[2 * pp + i], c) for i in range(2)], axis=0)
            grouped = _dot(perm_ref[...], _bf(rows))
            for j in range(CMP_STRIDE):
                xs[c, pair * pp:pair * (pp + 1), j * HEAD_DIM:(j + 1) * HEAD_DIM] = _bf(grouped[pair * j:pair * (j + 1)])
    kc = [_compress_tokens(xs[g], w1ab_ref.at[0], w2_ref.at[0], posrows_ref.at[0]) for g in range(NSA_KVH)]
    vc = [_compress_tokens(xs[NSA_KVH + g], w1ab_ref.at[1], w2_ref.at[1], posrows_ref.at[1])
          for g in range(NSA_KVH)]
    qbd = _bf(_qbd(q_ref[...], NSA_REP))
    s = _dot_nt(_bf(jnp.concatenate(kc, axis=1)), qbd) * SCALE
    tok = lax.broadcasted_iota(jnp.int32, s.shape, 0)
    p = _masked_softmax_rows(s, (tok * CMP_STRIDE + (CMP_LEN - 1) <= pos) & (tok < n_tok - 1))
    o_ref[...] = jnp.concatenate(
        [jnp.sum(p[:, h:h + 1] * vc[h // NSA_REP], axis=0, keepdims=True) for h in range(NSA_H)], axis=0)
    imp = _dot_f32_rhs(ovlt_ref[...], _dot_f32_lhs(p, grp_ref[...]))
    blk = lax.broadcasted_iota(jnp.int32, imp.shape, 0)
    cur = pos // SEL_BLOCK
    forced = (blk == 0) | (blk == cur) | (blk == cur - 1)
    score = jnp.where(blk <= cur, imp + jnp.where(forced, FORCE_BONUS, 0.0), NEG)
    n_sel = cur + 1
    rank = _rank_sublanes(score, n_sel)
    sel_ref[...] = jnp.where(rank < float(min(SEL_TOPK, n_sel)), 1.0, 0.0)


def _page_specs(n_pages, layer, heads, phased=False):
    def spec(p):
        if phased:
            return pl.BlockSpec((None, None, PAGE_SIZE, None, heads, HEAD_DIM),
                                lambda b, ph, pt: (layer, pt[b * n_pages + p], 0, ph, 0, 0))
        return pl.BlockSpec((None, None, PAGE_SIZE, 2, heads, HEAD_DIM),
                            lambda b, pt: (layer, pt[b * n_pages + p], 0, 0, 0, 0))
    return [spec(p) for p in range(n_pages)]


def _stream(ref, idx, first=0, count=None, step=1):
    rows = ref.shape[0]
    streams = int(np.prod(ref.shape[1:-1]))
    count = rows if count is None else count
    flat = ref.reshape(rows * streams, ref.shape[-1])
    return flat[pl.ds(first * streams + idx, count, stride=step * streams), :]


def _heads_on_lanes(ref, kind, heads):
    return jnp.concatenate([_stream(ref, kind * heads + h) for h in range(heads)], axis=1)


def nsa_cmp_decode(cache_cmp, layer, pt_flat, q_s, cw, n_pages):
    w1ab, w2, posrows = cw
    b = q_s.shape[0]
    pos = n_pages * PAGE_SIZE
    n_tok = n_pages * CMP_PER_PAGE
    cmp_start = np.arange(n_tok) * CMP_STRIDE
    sel_start = np.arange(SEL_ROWS) * SEL_BLOCK
    ovlt = ((cmp_start[None, :] < sel_start[:, None] + SEL_BLOCK) & (cmp_start[None, :] + CMP_LEN > sel_start[:, None])
            & (np.arange(n_tok)[None, :] < n_tok - 1))
    hh = np.arange(128)
    grp = (hh[:, None] // NSA_REP == hh[None, :] // NSA_REP) & (hh[:, None] < NSA_H) & (hh[None, :] < NSA_H)
    ovlt, grp = jnp.asarray(ovlt, jnp.bfloat16), jnp.asarray(grp, jnp.bfloat16)
    out_row = np.arange(2 * PAGE_SIZE)
    j_, page_, n_ = out_row // (2 * CMP_PER_PAGE), out_row // CMP_PER_PAGE % 2, out_row % CMP_PER_PAGE
    perm = jnp.asarray((page_ * PAGE_SIZE + n_ * CMP_STRIDE + j_)[:, None] == out_row[None, :], jnp.bfloat16)
    const = lambda a: pl.BlockSpec(a.shape, lambda i, pt: (0,) * a.ndim)
    grid_spec = pltpu.PrefetchScalarGridSpec(
        num_scalar_prefetch=1, grid=(b,),
        in_specs=_page_specs(n_pages, layer, NSA_KVH)
        + [pl.BlockSpec((None, NSA_H, HEAD_DIM), lambda i, pt: (i, 0, 0)),
           const(w1ab), const(w2), const(posrows), const(ovlt), const(grp), const(perm)],
        out_specs=[pl.BlockSpec((None, NSA_H, HEAD_DIM), lambda i, pt: (i, 0, 0)),
                   pl.BlockSpec((None, SEL_ROWS, 128), lambda i, pt: (i, 0, 0))],
        scratch_shapes=[pltpu.VMEM((2 * NSA_KVH, n_tok, CMP_STRIDE * HEAD_DIM), jnp.bfloat16)])
    return pl.pallas_call(
        functools.partial(_cmp_decode_kernel, n_pages=n_pages, pos=pos),
        grid_spec=grid_spec,
        out_shape=[jax.ShapeDtypeStruct((b, NSA_H, HEAD_DIM), jnp.float32),
                   jax.ShapeDtypeStruct((b, SEL_ROWS, 128), jnp.float32)],
        compiler_params=_cparams(("parallel",)),
        name="nsa_cmp_decode",
    )(pt_flat, *([cache_cmp] * n_pages), q_s, w1ab, w2, posrows, ovlt, grp, perm)


def _slc_decode_kernel(pt_ref, *refs, n_pages, pos):
    pages = refs[:n_pages]
    q_ref, kv_ref, sel_ref, o_ref, s_sc = refs[n_pages:]
    kw = NSA_KVH * HEAD_DIM
    qbd = _bf(_qbd(q_ref[...], NSA_REP))
    row = lax.broadcasted_iota(jnp.int32, (PAGE_SIZE, 128), 0)
    per_page = PAGE_SIZE // SEL_BLOCK
    for p in range(n_pages):
        s = _dot_nt(_bf(_heads_on_lanes(pages[p], 0, NSA_KVH)), qbd) * SCALE
        sel = sel_ref[pl.ds(per_page * p, 1), :]
        for i in range(1, per_page):
            sel = jnp.where(row >= i * SEL_BLOCK, sel_ref[pl.ds(per_page * p + i, 1), :], sel)
        kpos = p * PAGE_SIZE + row
        s_sc[p * PAGE_SIZE:(p + 1) * PAGE_SIZE, :] = jnp.where(kpos <= pos, jnp.where(sel > 0.5, s, NEG), NEG)
    k_new = kv_ref[:, 2 * kw:3 * kw]
    v_new = kv_ref[:, 3 * kw:4 * kw]
    s_new = _dot_nt(_bf(jnp.broadcast_to(k_new, (8, kw))), qbd)[0:1] * SCALE
    new_ok = sel_ref[pl.ds(pos // SEL_BLOCK, 1), :] > 0.5
    _softmax_keys_into(s_sc, s_new, new_ok, o_ref, pages, v_new, NSA_REP, PAGE_SIZE)


def _softmax_keys_into(s_sc, s_new, new_ok, o_ref, pages, v_new, heads_per_kv, rows):
    p_all, p_new = _softmax_keys(s_sc, s_new, new_ok)
    s_sc[...] = p_all
    n_heads = o_ref.shape[0]
    n_kv = n_heads // heads_per_kv
    acc = [jnp.zeros((n_heads, HEAD_DIM), jnp.float32)] * n_kv
    for p, page in enumerate(pages):
        p_bf = _bf(s_sc[p * rows:(p + 1) * rows, :])
        for g in range(n_kv):
            pv = lax.dot_general(p_bf, _bf(_stream(page, n_kv + g)), (((0,), (0,)), ((), ())),
                                 preferred_element_type=jnp.float32)
            acc[g] = acc[g] + pv[0:n_heads]
    head = lax.broadcasted_iota(jnp.int32, (n_heads, 1), 0)
    o = jnp.zeros((n_heads, HEAD_DIM), jnp.float32)
    for g in range(n_kv):
        new = jnp.concatenate([p_new[:, h:h + 1] for h in range(n_heads)], axis=0) * v_new[:, g * HEAD_DIM:(g + 1) * HEAD_DIM]
        o = o + jnp.where(head // heads_per_kv == g, acc[g] + new, 0.0)
    o_ref[...] = o.astype(o_ref.dtype)


def nsa_slc_decode(cache_slc, layer, pt_flat, q_s, kv_new, sel, n_pages):
    b = q_s.shape[0]
    pos = n_pages * PAGE_SIZE
    grid_spec = pltpu.PrefetchScalarGridSpec(
        num_scalar_prefetch=1, grid=(b,),
        in_specs=_page_specs(n_pages, layer, NSA_KVH)
        + [pl.BlockSpec((None, NSA_H, HEAD_DIM), lambda i, pt: (i, 0, 0)),
           pl.BlockSpec((None, 1, N_NKV), lambda i, pt: (i, 0, 0)),
           pl.BlockSpec((None, SEL_ROWS, 128), lambda i, pt: (i, 0, 0))],
        out_specs=pl.BlockSpec((None, NSA_H, HEAD_DIM), lambda i, pt: (i, 0, 0)),
        scratch_shapes=[pltpu.VMEM((n_pages * PAGE_SIZE, 128), jnp.float32)])
    return pl.pallas_call(
        functools.partial(_slc_decode_kernel, n_pages=n_pages, pos=pos),
        grid_spec=grid_spec,
        out_shape=jax.ShapeDtypeStruct((b, NSA_H, HEAD_DIM), jnp.float32),
        compiler_params=_cparams(("parallel",)),
        name="nsa_slc_decode",
    )(pt_flat, *([cache_slc] * n_pages), q_s, kv_new, sel)


def _win_decode_kernel(win_ref, q_ref, kv_ref, ocmp_ref, oslc_ref, gate_ref, *rest, pos, aliased):
    o_ref, wout_ref, s_sc, ow_sc = rest[1:] if aliased else rest
    kw = NSA_KVH * HEAD_DIM
    wb = win_ref.shape[0]
    wout_ref[pl.ds(0, wb - 1)] = win_ref[pl.ds(1, wb - 1)]
    for kind in range(2):
        for g in range(NSA_KVH):
            lane0 = (4 + kind) * kw + g * HEAD_DIM
            wout_ref[wb - 1, kind, pl.ds(g, 1), :] = kv_ref[:, lane0:lane0 + HEAD_DIM]
    qbd = _bf(_qbd(q_ref[...], NSA_REP))
    s = _dot_nt(_bf(_heads_on_lanes(win_ref, 0, NSA_KVH)), qbd) * SCALE
    dist = pos - (pos - wb + lax.broadcasted_iota(jnp.int32, s.shape, 0))
    s_sc[...] = jnp.where(dist >= 0, jnp.where(dist <= WINDOW, s, NEG), NEG)
    k_new = kv_ref[:, 4 * kw:5 * kw]
    v_new = kv_ref[:, 5 * kw:6 * kw]
    s_new = _dot_nt(_bf(jnp.broadcast_to(k_new, (8, kw))), qbd)[0:1] * SCALE
    new_ok = jnp.ones(s_new.shape, jnp.float32) > 0.5
    _softmax_keys_into(s_sc, s_new, new_ok, ow_sc, [win_ref], v_new, NSA_REP, wb)
    g = jax.nn.sigmoid(gate_ref[...])
    o_ref[...] = (g[0] * ocmp_ref[...] + g[1] * oslc_ref[...] + g[2] * ow_sc[...]).astype(o_ref.dtype)


def _layer_slot(prev):
    if prev is None:
        return [], [], False
    return [prev], [pl.BlockSpec(memory_space=pl.ANY)], True


def nsa_win_decode(cache_win, layer, q_s, kv_new, o_cmp, o_slc, gate_cols, pos, win_out_prev):
    b, wb = cache_win.shape[1], cache_win.shape[2]
    hspec = pl.BlockSpec((None, NSA_H, HEAD_DIM), lambda i: (i, 0, 0))
    wspec = pl.BlockSpec((None, None, wb, 2, NSA_KVH, HEAD_DIM), lambda i: (layer, i, 0, 0, 0, 0))
    extra, extra_specs, aliased = _layer_slot(win_out_prev)
    return pl.pallas_call(
        functools.partial(_win_decode_kernel, pos=pos, aliased=aliased),
        grid=(b,),
        in_specs=[wspec, hspec, pl.BlockSpec((None, 1, N_NKV), lambda i: (i, 0, 0)), hspec, hspec,
                  pl.BlockSpec((None, 3, NSA_H, 1), lambda i: (i, 0, 0, 0))] + extra_specs,
        out_specs=[hspec, wspec],
        out_shape=[jax.ShapeDtypeStruct((b, NSA_H, HEAD_DIM), jnp.bfloat16),
                   jax.ShapeDtypeStruct(cache_win.shape, cache_win.dtype)],
        scratch_shapes=[pltpu.VMEM((wb, 128), jnp.float32), pltpu.VMEM((NSA_H, HEAD_DIM), jnp.float32)],
        input_output_aliases={6: 1} if aliased else {},
        compiler_params=_cparams(("parallel",)),
        name="nsa_win_decode",
    )(cache_win, q_s, kv_new, o_cmp, o_slc, gate_cols, *extra)


def _moba_decode_kernel(pt_ref, *refs, n_pages, pos):
    pages = refs[:n_pages]
    q_ref, kn_ref, vn_ref, o_ref, s_sc, pn_sc = refs[n_pages:]
    phase = pl.program_id(1)
    per_blk = MOBA_BLOCK // PAGE_SIZE
    nb = n_pages // per_blk

    @pl.when(phase == 0)
    def _():
        qf = _qbd(q_ref[...], 1)
        qbd = _bf(qf)
        ksum = []
        for p in range(n_pages):
            k = jnp.concatenate([_stream(pages[p], h) for h in range(MOBA_H)], axis=1)
            s_sc[p * PAGE_SIZE:(p + 1) * PAGE_SIZE, :] = _dot_nt(_bf(k), qbd) * SCALE
            ksum.append(jnp.sum(k, axis=0, keepdims=True))
        kmean = jnp.concatenate(
            [sum(ksum[n * per_blk:(n + 1) * per_blk]) * (1.0 / MOBA_BLOCK) for n in range(nb)], axis=0)
        blk = lax.broadcasted_iota(jnp.int32, (nb, 128), 0)
        cur = pos // MOBA_BLOCK
        gate = jnp.where(blk < cur, _dot_nt_f32(kmean, qf), NEG)
        rank = _rank_sublanes(gate, nb)
        sel = jnp.where(rank < float(min(cur, MOBA_TOPK, nb)), 1.0, 0.0)
        row = lax.broadcasted_iota(jnp.int32, (PAGE_SIZE, 128), 0)
        for p in range(n_pages):
            ok = (jnp.broadcast_to(sel[p // per_blk:p // per_blk + 1, :], (PAGE_SIZE, 128)) > 0.5)
            kpos = p * PAGE_SIZE + row
            rows = slice(p * PAGE_SIZE, (p + 1) * PAGE_SIZE)
            s_sc[rows, :] = jnp.where(kpos <= pos, jnp.where(ok, s_sc[rows, :], NEG), NEG)
        s_new = _dot_nt(_bf(jnp.broadcast_to(kn_ref[...], (8, MIX_W))), qbd)[0:1] * SCALE
        new_ok = jnp.ones(s_new.shape, jnp.float32) > 0.5
        p_all, p_new = _softmax_keys(s_sc, s_new, new_ok)
        s_sc[...] = p_all
        pn_sc[...] = jnp.broadcast_to(p_new, pn_sc.shape)

    @pl.when(phase == 1)
    def _():
        acc = jnp.zeros((MOBA_H, MIX_W), jnp.float32)
        for p in range(n_pages):
            v_all = _bf(jnp.concatenate([_stream(pages[p], h) for h in range(MOBA_H)], axis=1))
            pv = lax.dot_general(_bf(s_sc[p * PAGE_SIZE:(p + 1) * PAGE_SIZE, :]), v_all, (((0,), (0,)), ((), ())),
                                 preferred_element_type=jnp.float32)
            acc = acc + pv[0:MOBA_H]
        p_new, v_new = pn_sc[0:1, :], vn_ref[...]
        o_ref[...] = jnp.concatenate(
            [acc[h:h + 1, h * HEAD_DIM:(h + 1) * HEAD_DIM]
             + p_new[:, h:h + 1] * v_new[:, h * HEAD_DIM:(h + 1) * HEAD_DIM] for h in range(MOBA_H)],
            axis=0).astype(o_ref.dtype)


def moba_decode(cache_moba, layer, pt_flat, q_m, k_new, v_new, n_pages):
    b = q_m.shape[0]
    pos = n_pages * PAGE_SIZE
    grid_spec = pltpu.PrefetchScalarGridSpec(
        num_scalar_prefetch=1, grid=(b, 2),
        in_specs=_page_specs(n_pages, layer, MOBA_H, phased=True)
        + [pl.BlockSpec((None, MOBA_H, HEAD_DIM), lambda i, ph, pt: (i, 0, 0)),
           pl.BlockSpec((None, 1, MIX_W), lambda i, ph, pt: (i, 0, 0)),
           pl.BlockSpec((None, 1, MIX_W), lambda i, ph, pt: (i, 0, 0))],
        out_specs=pl.BlockSpec((None, MOBA_H, HEAD_DIM), lambda i, ph, pt: (i, 0, 0)),
        scratch_shapes=[pltpu.VMEM((n_pages * PAGE_SIZE, 128), jnp.float32), pltpu.VMEM((8, 128), jnp.float32)])
    return pl.pallas_call(
        functools.partial(_moba_decode_kernel, n_pages=n_pages, pos=pos),
        grid_spec=grid_spec,
        out_shape=jax.ShapeDtypeStruct((b, MOBA_H, HEAD_DIM), jnp.bfloat16),
        compiler_params=_cparams(("parallel", "arbitrary")),
        name="moba_decode",
    )(pt_flat, *([cache_moba] * n_pages), q_m, k_new, v_new)


def _hgrn_decode_kernel(s_ref, hqf_ref, hi_ref, hg_ref, lb_ref, nw_ref, *rest, aliased):
    o_ref, so_ref = rest[1:] if aliased else rest
    mine = lax.broadcasted_iota(jnp.int32, hqf_ref.shape[2:], 1) == pl.program_id(0)
    col = lambda x: jnp.sum(jnp.where(mine, x, 0.0), axis=1, keepdims=True)
    for h in range(HG_H):
        lanes = slice(h * HG_DV, (h + 1) * HG_DV)
        q, k, logf = _hgrn_gates(col(hqf_ref[0, h]), col(hqf_ref[1, h]), lb_ref[h])
        s_new = jnp.exp(logf) * s_ref[h] + k * hi_ref[:, lanes]
        so_ref[h] = s_new
        o = jnp.sum(q * s_new, axis=0, keepdims=True)
        o_ref[:, lanes] = _hgrn_out(o, hg_ref[:, lanes], nw_ref[:, lanes]).astype(o_ref.dtype)


def hgrn_decode(state, layer, hqf_t, hi_row, hg_row, lb, norm_w, state_out_prev):
    b = state.shape[1]
    rowspec = pl.BlockSpec((None, 1, MIX_W), lambda i: (i, 0, 0))
    sspec = pl.BlockSpec((None, None, HG_H, HG_DK, HG_DV), lambda i: (layer, i, 0, 0, 0))
    extra, extra_specs, aliased = _layer_slot(state_out_prev)
    return pl.pallas_call(
        functools.partial(_hgrn_decode_kernel, aliased=aliased),
        grid=(b,),
        in_specs=[sspec, pl.BlockSpec(hqf_t.shape, lambda i: (0, 0, 0, 0)), rowspec, rowspec,
                  pl.BlockSpec((HG_H, HG_DK, 1), lambda i: (0, 0, 0)),
                  pl.BlockSpec((1, MIX_W), lambda i: (0, 0))] + extra_specs,
        out_specs=[rowspec, sspec],
        out_shape=[jax.ShapeDtypeStruct((b, 1, MIX_W), jnp.bfloat16),
                   jax.ShapeDtypeStruct(state.shape, jnp.float32)],
        input_output_aliases={6: 1} if aliased else {},
        compiler_params=_cparams(("parallel",)),
        name="hgrn_decode",
    )(state, hqf_t, hi_row, hg_row, lb.reshape(HG_H, HG_DK, 1), norm_w.reshape(1, MIX_W), *extra)


def finish_layer(x2, z, o_nsa, o_hg, o_moba, lw):
    layer = lw["layer"]
    mixed = branch_merge((o_nsa, o_hg, o_moba), lw["w_branch"], layer, z)
    x2 = matmul(mixed, lw["w_out"], layer, epilogue="residual", residual=x2, tn=512)
    h2 = rmsnorm_rows(x2, lw["norm2_w"], jnp.bfloat16)
    u2 = matmul(h2, lw["w_up"], layer, epilogue="relu2", out_dtype=jnp.bfloat16)
    return matmul(u2, lw["w_down"], layer, epilogue="residual", residual=x2)


def prompt_layer(x, lb, lw):
    b, t, _ = x.shape
    x2 = x.reshape(b * t, D_MODEL)
    h = rmsnorm_rows(x2, lw["norm1_w"], jnp.bfloat16)
    z = project_in(h, lw["w_in_t"], lw["layer"])
    ckv = nsa_compress_prompt(z, b, t, lw["cw"])
    o_nsa = nsa_prompt(z, ckv, b, t)
    o_moba = moba_prompt(z, b, t)
    o_hg, s_fin = hgrn_prompt(z, lb, lw["hg_norm_w"], b, t)
    x2 = finish_layer(x2, z, o_nsa, o_hg, o_moba, lw)
    return (x2.reshape(b, t, D_MODEL),) + kv_cache_rows(z, b, t) + (s_fin,)


def kv_cache_rows(z, b, t):
    z3 = z.reshape(b, t, D_IN_PAD)
    gw = 2 * NSA_KVH * HEAD_DIM
    kv = lambda kind: z3[:, :, OFF_NKV + kind * gw:OFF_NKV + (kind + 1) * gw].reshape(b, t, 2, NSA_KVH, HEAD_DIM)
    mkv = z3[:, :, OFF_MQKV + MIX_W:OFF_MQKV + 3 * MIX_W].reshape(b, t, 2, MOBA_H, HEAD_DIM)
    return kv(0), kv(1), mkv, kv(2)[:, max(t - WINDOW, 0):]


def sample_layer(x, layer, caches, page_table, lb, lw, stacked_prev):
    b, t, _ = x.shape
    assert t == 1, "the decode kernels take one new token per sequence"
    x2 = x.reshape(b * t, D_MODEL)
    h = rmsnorm_rows(x2, lw["norm1_w"], jnp.bfloat16)
    z = project_in(h, lw["w_in_t"], lw["layer"])
    o_nsa, o_hg, o_moba, win_out, s_out = sample_mixers(z, layer, caches, page_table, lb, lw, stacked_prev)
    x2 = finish_layer(x2, z, o_nsa, o_hg, o_moba, lw)
    c_cmp, c_slc, c_moba, _ = kv_cache_rows(z, b, t)
    return (x2.reshape(b, t, D_MODEL), c_cmp, c_slc, c_moba), (win_out, s_out)


def sample_mixers(z, layer, caches, page_table, lb, lw, stacked_prev=(None, None)):
    cache_cmp, cache_slc, cache_moba, cache_win, state = caches
    win_prev, state_prev = stacked_prev
    b = z.shape[0]
    n_pages = page_table.shape[1]
    pos = n_pages * PAGE_SIZE
    pt_flat = page_table.reshape(-1)
    col = lambda off, n: z[:, off:off + n]
    q_s = col(OFF_NQ, N_NQ).reshape(b, NSA_H, HEAD_DIM)
    kv_new = col(OFF_NKV, N_NKV).reshape(b, 1, N_NKV)
    gate_cols = col(OFF_NGATE, N_NGATE).reshape(b, 3, NSA_H, 1)
    o_cmp, sel = nsa_cmp_decode(cache_cmp, layer, pt_flat, q_s, lw["cw"], n_pages)
    o_slc = nsa_slc_decode(cache_slc, layer, pt_flat, q_s, kv_new, sel, n_pages)
    o_nsa, win_out = nsa_win_decode(cache_win, layer, q_s, kv_new, o_cmp, o_slc, gate_cols, pos, win_prev)
    q_m = col(OFF_MQKV, MIX_W).reshape(b, MOBA_H, HEAD_DIM)
    k_new = col(OFF_MQKV + MIX_W, MIX_W).reshape(b, 1, MIX_W)
    v_new = col(OFF_MQKV + 2 * MIX_W, MIX_W).reshape(b, 1, MIX_W)
    o_moba = moba_decode(cache_moba, layer, pt_flat, q_m, k_new, v_new, n_pages)
    hqf_t = col(OFF_HQ, 2 * N_HG).T.reshape(2, HG_H, HG_DK, b)
    o_hg, s_out = hgrn_decode(state, layer, hqf_t, col(OFF_HI, N_HG).reshape(b, 1, MIX_W),
                              col(OFF_HGATE, N_HG).reshape(b, 1, MIX_W), lb, lw["hg_norm_w"], state_prev)
    return o_nsa.reshape(b, MIX_W), o_hg.reshape(b, MIX_W), o_moba.reshape(b, MIX_W), win_out, s_out


def prep_layer_weights(i, norm1_w, norm2_w, w_in, nsa_cmp_pos, nsa_cmp_w1, nsa_cmp_w2, hgrn_norm_w,
                       w_branch, w_out, w_up, w_down):
    assert w_in.shape[2] - (N_NQ + N_NKV + N_NGATE) == OFF_NGATE - (N_NQ + N_NKV)
    return dict(layer=i, norm1_w=norm1_w[i], norm2_w=norm2_w[i], w_in_t=jnp.swapaxes(w_in, 1, 2),
                cw=prep_compress_weights(nsa_cmp_pos[i], nsa_cmp_w1[i], nsa_cmp_w2[i]), hg_norm_w=hgrn_norm_w[i],
                w_branch=w_branch, w_out=w_out, w_up=w_up, w_down=w_down.astype(jnp.bfloat16))


def kernel(x_prompt, x_sample, cache_nsa_cmp, cache_nsa_slc, cache_moba, cache_nsa_win, state_hgrn,
           page_table, norm1_w, norm2_w, w_in, nsa_cmp_pos, nsa_cmp_w1, nsa_cmp_w2, hgrn_lb_logits,
           hgrn_norm_w, w_branch, w_out, w_up, w_down, final_norm_w):
    sm = jax.nn.softmax(hgrn_lb_logits.astype(jnp.float32), axis=0)
    lbs = jnp.cumsum(sm, axis=0) - sm[0:1]
    xp, xs = x_prompt, x_sample
    outs_p, outs_s = [], []
    stacked = (None, None)
    for i in range(DEPTH):
        lw = prep_layer_weights(i, norm1_w, norm2_w, w_in, nsa_cmp_pos, nsa_cmp_w1, nsa_cmp_w2,
                                hgrn_norm_w, w_branch, w_out, w_up, w_down)
        xp, *st_p = prompt_layer(xp, lbs[i], lw)
        (xs, *st_s), stacked = sample_layer(
            xs, i, (cache_nsa_cmp, cache_nsa_slc, cache_moba, cache_nsa_win, state_hgrn), page_table, lbs[i], lw,
            stacked)
        outs_p.append(st_p)
        outs_s.append(st_s)
    y_prompt = rmsnorm_rows(xp.reshape(-1, D_MODEL), final_norm_w, jnp.float32).reshape(xp.shape)
    y_sample = rmsnorm_rows(xs.reshape(-1, D_MODEL), final_norm_w, jnp.float32).reshape(xs.shape)
    stack = lambda outs, j: jnp.stack([o[j] for o in outs])
    return (y_prompt, y_sample) + tuple(stack(outs_p, j) for j in range(5)) + tuple(
        stack(outs_s, j) for j in range(3)) + stacked
```
